```python
import math
import jax, jax.numpy as jnp
from jax import lax
import numpy as np

D_MODEL = 4096
BATCH = 4
SEQ = 2048
DEPTH = 2
DEC_BATCH = 8
DEC_SEQ = 4
PAST_LEN = 16384
PAGE_SIZE = 128

N_HEADS = 16
N_KV_HEADS = 4
HEAD_DIM = 128
GROUP = N_HEADS // N_KV_HEADS
ATTN_WIDTH = N_HEADS * HEAD_DIM
CONV_CH = D_MODEL - ATTN_WIDTH
KV_WIDTH = N_KV_HEADS * HEAD_DIM
CONV_WIDTH = 31
ROT_DIM = HEAD_DIM // 4
ROPE_THETA = 500000.0
CMP_LEN = 32
CMP_STRIDE = 16
CMP_HIDDEN = 2 * HEAD_DIM
SLC_LEN = 64
N_SEL = 16
WINDOW = 512
WIN_BLOCK = 128
SEL_Q_BLOCK = 32
D_FF = -(-8 * D_MODEL // (3 * 256)) * 256
Q_END = ATTN_WIDTH
KV_END = Q_END + 6 * KV_WIDTH
GATE_END = KV_END + 3 * N_HEADS
IN_COLS = GATE_END + 2 * CONV_CH
SCALE = HEAD_DIM ** -0.5
NEG = -1e30
FORCE = 1e9
RMS_EPS = 1e-6
LN_EPS = 1e-5

kernel_name = "nsa_conformer_hybrid_step"


def rms_norm(x, g):
    xf = x.astype(jnp.float32)
    y = xf * lax.rsqrt(jnp.mean(xf * xf, axis=-1, keepdims=True) + RMS_EPS)
    return (y * g.astype(jnp.float32)).astype(x.dtype)


def rope(x, pos):
    half = ROT_DIM // 2
    inv = ROPE_THETA ** (-jnp.arange(half, dtype=jnp.float32) / half)
    ang = pos.astype(jnp.float32)[:, None] * inv[None, :]
    shape = (pos.shape[0],) + (1,) * (x.ndim - 3) + (half,)
    cos = jnp.cos(ang).reshape(shape).astype(x.dtype)
    sin = jnp.sin(ang).reshape(shape).astype(x.dtype)
    x1 = x[..., :half]
    x2 = x[..., half:ROT_DIM]
    return jnp.concatenate([x1 * cos - x2 * sin, x2 * cos + x1 * sin, x[..., ROT_DIM:]], axis=-1)


def compress_blocks(x, pe, w1, w2, b2):
    bsz, s_len, n_kv, d = x.shape
    n_chunks = s_len // CMP_STRIDE
    ratio = CMP_LEN // CMP_STRIDE
    n_blk = n_chunks - ratio + 1
    chunks = x[:, :n_chunks * CMP_STRIDE].reshape(bsz, n_chunks, CMP_STRIDE, n_kv, d)
    pre = jnp.einsum('jd,jdf->f', pe, w1)
    for r in range(ratio):
        pre = pre + jnp.einsum('bcjhd,jdf->bchf', chunks[:, r:r + n_blk],
                               w1[r * CMP_STRIDE:(r + 1) * CMP_STRIDE])
    hidden = jax.nn.gelu(pre)
    return jnp.einsum('bchf,fd->bchd', hidden, w2) + b2


def to_slc_blocks(x, n_slc):
    bsz, s_len, n_kv, d = x.shape
    x = jnp.pad(x, ((0, 0), (0, n_slc * SLC_LEN - s_len), (0, 0), (0, 0)))
    return x.reshape(bsz, n_slc, SLC_LEN, n_kv, d).transpose(0, 3, 1, 2, 4)


def nsa_compressed_selected(q, full_kv, q_pos, pe, w1, w2, b2):
    bsz, t_len = q.shape[0], q.shape[1]
    s_len = full_kv.shape[1]
    kc = compress_blocks(full_kv[:, :, 0], pe[0], w1[0], w2[0], b2[0])
    vc = compress_blocks(full_kv[:, :, 1], pe[1], w1[1], w2[1], b2[1])
    n_cmp = kc.shape[1]
    blk_end = jnp.arange(n_cmp) * CMP_STRIDE + CMP_LEN - 1
    n_slc = -(-s_len // SLC_LEN)
    ks_blocks = to_slc_blocks(full_kv[:, :, 2], n_slc)
    vs_blocks = to_slc_blocks(full_kv[:, :, 3], n_slc)
    ci = jnp.arange(n_cmp)[:, None] * CMP_STRIDE
    sj = jnp.arange(n_slc)[None, :] * SLC_LEN
    overlap = ((ci < sj + SLC_LEN) & (ci + CMP_LEN > sj)).astype(jnp.float32)
    n_sel = min(N_SEL, n_slc)
    tc = math.gcd(t_len, SEL_Q_BLOCK)
    nc = t_len // tc
    qg = q.reshape(bsz, nc, tc, N_KV_HEADS, GROUP, HEAD_DIM).transpose(1, 0, 2, 3, 4, 5)
    pos_c = q_pos.reshape(nc, tc)
    blk_idx = jnp.arange(n_slc)
    gather = jax.vmap(jax.vmap(lambda blk, i: blk[i]))

    def one_chunk(args):
        qc, pc = args
        s = jnp.einsum('bqkgd,bnkd->bqkgn', qc, kc).astype(jnp.float32) * SCALE
        vmask = (blk_end[None, :] <= pc[:, None])[None, :, None, None, :]
        p = jax.nn.softmax(jnp.where(vmask, s, NEG), axis=-1) * vmask
        o_cmp = jnp.einsum('bqkgn,bnkd->bqkgd', p.astype(vc.dtype), vc)
        imp = jnp.einsum('bqkn,ns->bqks', p.sum(axis=3), overlap)
        cur = pc // SLC_LEN
        forced = (blk_idx[None] == 0) | (blk_idx[None] == cur[:, None]) | (blk_idx[None] == cur[:, None] - 1)
        allowed = blk_idx[None] <= cur[:, None]
        score = jnp.where(allowed[None, :, None, :],
                          jnp.where(forced[None, :, None, :], FORCE, imp), -1.0)
        _, idx = lax.top_k(score, n_sel)
        idx_t = idx.transpose(0, 2, 1, 3)
        ksel = gather(ks_blocks, idx_t)
        vsel = gather(vs_blocks, idx_t)
        s2 = jnp.einsum('bqkgd,bkqsjd->bqkgsj', qc, ksel).astype(jnp.float32) * SCALE
        kpos = (idx_t[..., None] * SLC_LEN + jnp.arange(SLC_LEN)).transpose(0, 2, 1, 3, 4)
        m2 = kpos[:, :, :, None] <= pc[None, :, None, None, None, None]
        s2 = jnp.where(m2, s2, NEG)
        p2 = jax.nn.softmax(s2.reshape(s2.shape[:4] + (-1,)), axis=-1).reshape(s2.shape)
        o_slc = jnp.einsum('bqkgsj,bkqsjd->bqkgd', p2.astype(vsel.dtype), vsel)
        return o_cmp, o_slc

    o_cmp, o_slc = lax.map(one_chunk, (qg, pos_c))
    o_cmp = o_cmp.transpose(1, 0, 2, 3, 4, 5).reshape(bsz, t_len, N_HEADS, HEAD_DIM)
    o_slc = o_slc.transpose(1, 0, 2, 3, 4, 5).reshape(bsz, t_len, N_HEADS, HEAD_DIM)
    return o_cmp, o_slc


def banded_window_attention(q, k, v):
    bsz, t_len = q.shape[0], q.shape[1]
    nb = t_len // WIN_BLOCK
    nprev = WINDOW // WIN_BLOCK
    span = (nprev + 1) * WIN_BLOCK

    def bands(x):
        xp = jnp.pad(x, ((0, 0), (WINDOW, 0), (0, 0), (0, 0)))
        xp = xp.reshape(bsz, nprev + nb, WIN_BLOCK, N_KV_HEADS, HEAD_DIM)
        return jnp.concatenate([xp[:, i:i + nb] for i in range(nprev + 1)], axis=2)

    kw, vw = bands(k), bands(v)
    qb = q.reshape(bsz, nb, WIN_BLOCK, N_KV_HEADS, GROUP, HEAD_DIM)
    s = jnp.einsum('bnqkgd,bnskd->bnkgqs', qb, kw).astype(jnp.float32) * SCALE
    qpos = jnp.arange(nb)[:, None] * WIN_BLOCK + jnp.arange(WIN_BLOCK)[None, :]
    kpos = jnp.arange(nb)[:, None] * WIN_BLOCK - WINDOW + jnp.arange(span)[None, :]
    diff = qpos[:, :, None] - kpos[:, None, :]
    mask = (diff >= 0) & (diff < WINDOW) & (kpos[:, None, :] >= 0)
    p = jax.nn.softmax(jnp.where(mask[None, :, None, None], s, NEG), axis=-1)
    o = jnp.einsum('bnkgqs,bnskd->bnqkgd', p.astype(vw.dtype), vw)
    return o.reshape(bsz, t_len, N_HEADS, HEAD_DIM)


def window_attention(q, k, v, q_pos, k_pos):
    bsz, t_len = q.shape[0], q.shape[1]
    qg = q.reshape(bsz, t_len, N_KV_HEADS, GROUP, HEAD_DIM)
    s = jnp.einsum('btkgd,bskd->btkgs', qg, k).astype(jnp.float32) * SCALE
    diff = q_pos[:, None] - k_pos[None, :]
    mask = (diff >= 0) & (diff < WINDOW)
    p = jax.nn.softmax(jnp.where(mask[None, :, None, None, :], s, NEG), axis=-1)
    o = jnp.einsum('btkgs,bskd->btkgd', p.astype(v.dtype), v)
    return o.reshape(bsz, t_len, N_HEADS, HEAD_DIM)


def conformer_conv(u, buf, dw_w, dw_b, ln_g, ln_b):
    a = u[..., :CONV_CH] * jax.nn.sigmoid(u[..., CONV_CH:])
    ext = jnp.concatenate([buf.astype(a.dtype), a], axis=1)
    y = lax.conv_general_dilated(ext, dw_w[:, None, :].astype(a.dtype), window_strides=(1,),
                                 padding='VALID', dimension_numbers=('NWC', 'WIO', 'NWC'),
                                 feature_group_count=CONV_CH) + dw_b
    yf = y.astype(jnp.float32)
    mu = jnp.mean(yf, axis=-1, keepdims=True)
    var = jnp.mean(jnp.square(yf - mu), axis=-1, keepdims=True)
    yn = ((yf - mu) * lax.rsqrt(var + LN_EPS) * ln_g.astype(jnp.float32) + ln_b.astype(jnp.float32)).astype(a.dtype)
    return jax.nn.silu(yn), ext[:, ext.shape[1] - (CONV_WIDTH - 1):]


def hybrid_layer(x, q_pos, lp, past_kv, past_win, past_conv):
    bsz, t_len, _ = x.shape
    xn = rms_norm(x, lp['norm_mix'])
    z = xn @ lp['w_in']
    q = rope(z[..., :Q_END].reshape(bsz, t_len, N_HEADS, HEAD_DIM), q_pos)
    kv = z[..., Q_END:KV_END].reshape(bsz, t_len, 6, N_KV_HEADS, HEAD_DIM)
    kv = kv.at[:, :, 0::2].set(rope(kv[:, :, 0::2], q_pos))
    gates = jax.nn.sigmoid(z[..., KV_END:GATE_END].astype(jnp.float32)).astype(x.dtype)
    gates = gates.reshape(bsz, t_len, 3, N_HEADS)
    u = z[..., GATE_END:]
    new_kv = kv[:, :, :4]
    full_kv = new_kv if past_kv is None else jnp.concatenate([past_kv, new_kv], axis=1)
    o_cmp, o_slc = nsa_compressed_selected(q, full_kv, q_pos, lp['cmp_pe'], lp['cmp_w1'],
                                           lp['cmp_w2'], lp['cmp_b2'])
    win_new = kv[:, :, 4:]
    if past_win is None:
        o_win = banded_window_attention(q, win_new[:, :, 0], win_new[:, :, 1])
        win_all = win_new
    else:
        win_all = jnp.concatenate([past_win, win_new], axis=1)
        k_pos = q_pos[0] - past_win.shape[1] + jnp.arange(win_all.shape[1])
        o_win = window_attention(q, win_all[:, :, 0], win_all[:, :, 1], q_pos, k_pos)
    keep = min(WINDOW, win_all.shape[1])
    new_win = win_all[:, win_all.shape[1] - keep:]
    o_attn = (gates[:, :, 0, :, None] * o_cmp + gates[:, :, 1, :, None] * o_slc
              + gates[:, :, 2, :, None] * o_win)
    buf = jnp.zeros((bsz, CONV_WIDTH - 1, CONV_CH), x.dtype) if past_conv is None else past_conv
    o_conv, new_conv = conformer_conv(u, buf, lp['conv_dw_w'], lp['conv_dw_b'],
                                      lp['conv_ln_g'], lp['conv_ln_b'])
    mixed = jnp.concatenate([o_attn.reshape(bsz, t_len, ATTN_WIDTH), o_conv], axis=-1)
    h = x + mixed @ lp['w_out']
    hn = rms_norm(h, lp['norm_ffn'])
    h = h + (jax.nn.silu(hn @ lp['w_gate']) * (hn @ lp['w_up'])) @ lp['w_down']
    return h, new_kv, new_win, new_conv


def setup_inputs(seed: int = 0) -> dict:
    key = jax.random.key(seed)
    ks = jax.random.split(key, 24)
    n_pages = PAST_LEN // PAGE_SIZE
    used = DEC_BATCH * n_pages
    n_pool = used + max(1, used // 4)
    win_rows = min(WINDOW, PAST_LEN)
    nrm = jax.random.normal
    f32 = jnp.float32
    page_table = jax.random.permutation(ks[0], n_pool)[:used].reshape(DEC_BATCH, n_pages).astype(jnp.int32)
    return {
        'x_prompt': nrm(ks[1], (BATCH, SEQ, D_MODEL), f32),
        'x_sample': nrm(ks[2], (DEC_BATCH, DEC_SEQ, D_MODEL), f32),
        'cache_kv': nrm(ks[3], (DEPTH, n_pool, PAGE_SIZE, 4, N_KV_HEADS, HEAD_DIM), f32),
        'cache_win': nrm(ks[4], (DEPTH, DEC_BATCH, win_rows, 2, N_KV_HEADS, HEAD_DIM), f32),
        'state_conv': 0.5 * nrm(ks[5], (DEPTH, DEC_BATCH, CONV_WIDTH - 1, CONV_CH), f32),
        'page_table': page_table,
        'norm_mix': 1.0 + 0.02 * nrm(ks[6], (DEPTH, D_MODEL), f32),
        'w_in': nrm(ks[7], (DEPTH, D_MODEL, IN_COLS), f32) * D_MODEL ** -0.5,
        'cmp_pe': 0.1 * nrm(ks[8], (DEPTH, 2, CMP_LEN, HEAD_DIM), f32),
        'cmp_w1': nrm(ks[9], (DEPTH, 2, CMP_LEN, HEAD_DIM, CMP_HIDDEN), f32) * (CMP_LEN * HEAD_DIM) ** -0.5,
        'cmp_w2': nrm(ks[10], (DEPTH, 2, CMP_HIDDEN, HEAD_DIM), f32) * CMP_HIDDEN ** -0.5,
        'cmp_b2': 0.02 * nrm(ks[11], (DEPTH, 2, HEAD_DIM), f32),
        'conv_dw_w': nrm(ks[12], (DEPTH, CONV_WIDTH, CONV_CH), f32) * CONV_WIDTH ** -0.5,
        'conv_dw_b': 0.02 * nrm(ks[13], (DEPTH, CONV_CH), f32),
        'conv_ln_g': 1.0 + 0.02 * nrm(ks[14], (DEPTH, CONV_CH), f32),
        'conv_ln_b': 0.02 * nrm(ks[15], (DEPTH, CONV_CH), f32),
        'w_out': nrm(ks[16], (DEPTH, ATTN_WIDTH + CONV_CH, D_MODEL), f32) * (ATTN_WIDTH + CONV_CH) ** -0.5,
        'norm_ffn': 1.0 + 0.02 * nrm(ks[17], (DEPTH, D_MODEL), f32),
        'w_gate': nrm(ks[18], (DEPTH, D_MODEL, D_FF), f32) * D_MODEL ** -0.5,
        'w_up': nrm(ks[19], (DEPTH, D_MODEL, D_FF), f32) * D_MODEL ** -0.5,
        'w_down': nrm(ks[20], (DEPTH, D_FF, D_MODEL), f32) * D_FF ** -0.5,
        'norm_final': 1.0 + 0.02 * nrm(ks[21], (D_MODEL,), f32),
    }


def reference(x_prompt, x_sample, cache_kv, cache_win, state_conv, page_table, norm_mix, w_in,
              cmp_pe, cmp_w1, cmp_w2, cmp_b2, conv_dw_w, conv_dw_b, conv_ln_g, conv_ln_b,
              w_out, norm_ffn, w_gate, w_up, w_down, norm_final):
    dec_b, n_pages = page_table.shape
    page_size = cache_kv.shape[2]
    past_len = n_pages * page_size
    pos_prompt = jnp.arange(x_prompt.shape[1], dtype=jnp.int32)
    pos_sample = past_len + jnp.arange(x_sample.shape[1], dtype=jnp.int32)
    hp, hs = x_prompt, x_sample
    kvp, winp, convp, kvs, wins, convs = [], [], [], [], [], []
    for l in range(DEPTH):
        lp = {'norm_mix': norm_mix[l], 'w_in': w_in[l], 'cmp_pe': cmp_pe[l], 'cmp_w1': cmp_w1[l],
              'cmp_w2': cmp_w2[l], 'cmp_b2': cmp_b2[l], 'conv_dw_w': conv_dw_w[l],
              'conv_dw_b': conv_dw_b[l], 'conv_ln_g': conv_ln_g[l], 'conv_ln_b': conv_ln_b[l],
              'w_out': w_out[l], 'norm_ffn': norm_ffn[l], 'w_gate': w_gate[l], 'w_up': w_up[l],
              'w_down': w_down[l]}
        hp, a, b, c = hybrid_layer(hp, pos_prompt, lp, None, None, None)
        kvp.append(a); winp.append(b); convp.append(c)
        past_kv = cache_kv[l][page_table].reshape((dec_b, past_len) + cache_kv.shape[3:])
        hs, a, b, c = hybrid_layer(hs, pos_sample, lp, past_kv, cache_win[l], state_conv[l])
        kvs.append(a); wins.append(b); convs.append(c)
    y_prompt = rms_norm(hp, norm_final)
    y_sample = rms_norm(hs, norm_final)
    return (y_prompt, y_sample, jnp.stack(kvp), jnp.stack(winp), jnp.stack(convp),
            jnp.stack(kvs), jnp.stack(wins), jnp.stack(convs))
```

```python
import functools
import math

import jax
import jax.numpy as jnp
from jax import lax
from jax.experimental import pallas as pl
from jax.experimental.pallas import tpu as pltpu

F32 = jnp.float32
BF16 = jnp.bfloat16
I32 = jnp.int32

N_HEADS = 16
N_KV = 4
HEAD_DIM = 128
GROUP = N_HEADS // N_KV
ATTN_WIDTH = N_HEADS * HEAD_DIM
KV_WIDTH = N_KV * HEAD_DIM
CONV_CH = 2048
CONV_WIDTH = 31
ROT_DIM = HEAD_DIM // 4
ROPE_THETA = 500000.0
CMP_LEN = 32
CMP_STRIDE = 16
CMP_HIDDEN = 2 * HEAD_DIM
SLC_LEN = 64
SLC_SHIFT = 6
N_SEL = 16
WINDOW = 512
SCALE = HEAD_DIM ** -0.5
NEG = -1e30
FORCE = 1e9
RMS_EPS = 1e-6
LN_EPS = 1e-5

LANES = 128
SUBLANES = 8
VMEM_LIMIT = 56 * 1024 * 1024

PAGE = 128
CHUNKS_PER_PAGE = PAGE // CMP_STRIDE
COL_U = 0
COL_Q = 2 * CONV_CH
COL_KV = COL_Q + ATTN_WIDTH
KV_COLS = 6 * KV_WIDTH
PROJ_TN = 512


def _cparams(sem):
    return pltpu.CompilerParams(dimension_semantics=sem, vmem_limit_bytes=VMEM_LIMIT)


def _dot(a, b):
    return jnp.dot(a, b, preferred_element_type=F32)


def _dot_nt(a, b):
    return lax.dot_general(a, b, (((1,), (1,)), ((), ())), preferred_element_type=F32)


def _split3_dot(x, m_bf16):
    hi = x.astype(BF16)
    r1 = x - hi.astype(F32)
    mid = r1.astype(BF16)
    lo = (r1 - mid.astype(F32)).astype(BF16)
    return _dot(hi, m_bf16) + _dot(mid, m_bf16) + _dot(lo, m_bf16)


def _rmsnorm_kernel(x_ref, g_ref, o_ref):
    x = x_ref[...]
    ms = jnp.mean(x * x, axis=-1, keepdims=True)
    o_ref[...] = (x * lax.rsqrt(ms + RMS_EPS) * g_ref[...]).astype(o_ref.dtype)


def _rmsnorm(x, g, out_dtype):
    m, d = x.shape
    tm = min(m, 256)
    return pl.pallas_call(
        _rmsnorm_kernel,
        out_shape=jax.ShapeDtypeStruct((m, d), out_dtype),
        grid=(m // tm,),
        in_specs=[pl.BlockSpec((tm, d), lambda i: (i, 0)),
                  pl.BlockSpec((1, d), lambda i: (0, 0))],
        out_specs=pl.BlockSpec((tm, d), lambda i: (i, 0)),
        compiler_params=_cparams(("parallel",)),
        name="rmsnorm",
    )(x, g.reshape(1, d))


def _inproj_kernel(a_ref, w_ref, wg_ref, c_ref, s1_ref, s2_ref, z_ref, g_ref):
    j = pl.program_id(1)
    a = a_ref[...]
    acc = _dot(a, w_ref[...])
    q0 = COL_Q // PROJ_TN
    k0 = COL_KV // PROJ_TN
    is_rope = ((j >= q0) & (j <= k0)) | (j == k0 + 2) | (j == k0 + 4)

    @pl.when(is_rope)
    def _():
        c = c_ref[...]
        s1 = s1_ref[...]
        s2 = s2_ref[...]
        for h in range(PROJ_TN // HEAD_DIM):
            x = acc[:, h * HEAD_DIM:(h + 1) * HEAD_DIM]
            z_ref[:, h * HEAD_DIM:(h + 1) * HEAD_DIM] = (
                x * c + pltpu.roll(x, ROT_DIM // 2, 1) * s1
                + pltpu.roll(x, HEAD_DIM - ROT_DIM // 2, 1) * s2)

    @pl.when(jnp.logical_not(is_rope))
    def _():
        z_ref[...] = acc

    @pl.when(j == 0)
    def _():
        g_ref[...] = jax.nn.sigmoid(_dot(a, wg_ref[...]))


def _inproj(xn, w, wg, tabs, tab_tiles, tm):
    m, k = xn.shape
    n = w.shape[1]
    c, s1, s2 = tabs
    tab_spec = pl.BlockSpec((tm, HEAD_DIM), lambda i, j: (i % tab_tiles, 0))
    return pl.pallas_call(
        _inproj_kernel,
        out_shape=(jax.ShapeDtypeStruct((m, n), F32),
                   jax.ShapeDtypeStruct((m, N_KV * LANES), F32)),
        grid=(m // tm, n // PROJ_TN),
        in_specs=[pl.BlockSpec((tm, k), lambda i, j: (i, 0)),
                  pl.BlockSpec((k, PROJ_TN), lambda i, j: (0, j)),
                  pl.BlockSpec((k, N_KV * LANES), lambda i, j: (0, 0)),
                  tab_spec, tab_spec, tab_spec],
        out_specs=(pl.BlockSpec((tm, PROJ_TN), lambda i, j: (i, j)),
                   pl.BlockSpec((tm, N_KV * LANES), lambda i, j: (i, 0))),
        compiler_params=_cparams(("parallel", "arbitrary")),
        name="inproj",
    )(xn, w, wg, c, s1, s2)


def _outproj_kernel(a1_ref, a2_ref, w1_ref, w2_ref, r_ref, o_ref):
    o_ref[...] = r_ref[...] + _dot(a1_ref[...], w1_ref[...]) + _dot(a2_ref[...], w2_ref[...])


def _outproj(a1, a2, w, res, tm, tn):
    m, k1 = a1.shape
    k2 = a2.shape[1]
    n = w.shape[1]
    assert k1 == k2 and w.shape[0] == k1 + k2
    return pl.pallas_call(
        _outproj_kernel,
        out_shape=jax.ShapeDtypeStruct((m, n), F32),
        grid=(m // tm, n // tn),
        in_specs=[pl.BlockSpec((tm, k1), lambda i, j: (i, 0)),
                  pl.BlockSpec((tm, k2), lambda i, j: (i, 0)),
                  pl.BlockSpec((k1, tn), lambda i, j: (0, j)),
                  pl.BlockSpec((k2, tn), lambda i, j: (1, j)),
                  pl.BlockSpec((tm, tn), lambda i, j: (i, j))],
        out_specs=pl.BlockSpec((tm, tn), lambda i, j: (i, j)),
        compiler_params=_cparams(("parallel", "arbitrary")),
        name="outproj",
    )(a1, a2, w, w, res)


def _ffn_up_kernel(a_ref, wg_ref, wu_ref, o_ref):
    a = a_ref[...]
    gate = _dot(a, wg_ref[...])
    up = _dot(a, wu_ref[...])
    o_ref[...] = (gate * jax.nn.sigmoid(gate) * up).astype(o_ref.dtype)


def _ffn_up(a, wg, wu, tm, tn):
    m, k = a.shape
    n = wg.shape[1]
    return pl.pallas_call(
        _ffn_up_kernel,
        out_shape=jax.ShapeDtypeStruct((m, n), BF16),
        grid=(m // tm, n // tn),
        in_specs=[pl.BlockSpec((tm, k), lambda i, j: (i, 0)),
                  pl.BlockSpec((k, tn), lambda i, j: (0, j)),
                  pl.BlockSpec((k, tn), lambda i, j: (0, j))],
        out_specs=pl.BlockSpec((tm, tn), lambda i, j: (i, j)),
        compiler_params=_cparams(("parallel", "arbitrary")),
        name="ffn_up",
    )(a, wg, wu)


def _ffn_down_kernel(a_ref, w_ref, r_ref, o_ref, acc_ref):
    kk = pl.program_id(2)

    @pl.when(kk == 0)
    def _():
        acc_ref[...] = r_ref[...]

    acc_ref[...] += _dot(a_ref[...], w_ref[...])

    @pl.when(kk == pl.num_programs(2) - 1)
    def _():
        o_ref[...] = acc_ref[...]


def _ffn_down(a, w, res, tm, tn, tk):
    m, k = a.shape
    n = w.shape[1]
    return pl.pallas_call(
        _ffn_down_kernel,
        out_shape=jax.ShapeDtypeStruct((m, n), F32),
        grid=(m // tm, n // tn, k // tk),
        in_specs=[pl.BlockSpec((tm, tk), lambda i, j, kk: (i, kk)),
                  pl.BlockSpec((tk, tn), lambda i, j, kk: (kk, j)),
                  pl.BlockSpec((tm, tn), lambda i, j, kk: (i, j))],
        out_specs=pl.BlockSpec((tm, tn), lambda i, j, kk: (i, j)),
        scratch_shapes=[pltpu.VMEM((tm, tn), F32)],
        compiler_params=_cparams(("parallel", "parallel", "arbitrary")),
        name="ffn_down",
    )(a, w, res)


def _gelu_tanh(x):
    return 0.5 * x * (1.0 + jnp.tanh(math.sqrt(2.0 / math.pi) * (x + 0.044715 * (x * x * x))))


def _compress_kernel(*refs, n_pages, n_prefetch):
    page_refs = refs[n_prefetch:n_prefetch + n_pages]
    pef_ref, w1_ref, w2_ref, b2_ref, kc_ref, vc_ref, carry_ref = refs[n_prefetch + n_pages:]
    grp = pl.program_id(1)
    n_chunks = n_pages * CHUNKS_PER_PAGE
    half_k = CMP_STRIDE * HEAD_DIM

    @pl.when(grp == 0)
    def _():
        carry_ref[...] = jnp.zeros_like(carry_ref)

    first = lax.broadcasted_iota(I32, (n_chunks, CMP_HIDDEN), 0) == 0
    pages = [p.reshape(PAGE * 2 * N_KV, HEAD_DIM) for p in page_refs]
    for kind, out_ref in ((0, kc_ref), (1, vc_ref)):
        cols = []
        for j in range(CMP_STRIDE):
            cols.append(jnp.concatenate(
                [p[pl.ds(j * 2 * N_KV + kind * N_KV + h, CHUNKS_PER_PAGE, stride=CMP_STRIDE * 2 * N_KV), :]
                 for h in range(N_KV) for p in pages], axis=0).astype(BF16))
        x_all = jnp.concatenate(cols, axis=1)
        a0 = _dot(x_all, w1_ref[kind, 0])
        a1 = _dot(x_all, w1_ref[kind, 1])
        pe = pef_ref[kind]
        pe_term = (_dot(pe[:, :half_k], w1_ref[kind, 0]) + _dot(pe[:, half_k:], w1_ref[kind, 1]))[0:1, :]
        outs = []
        for h in range(N_KV):
            r0 = h * n_chunks
            prev_last = carry_ref[kind, r0 + n_chunks - 1:r0 + n_chunks, :]
            shifted = jnp.where(first, prev_last, pltpu.roll(a0[r0:r0 + n_chunks], 1, 0))
            pre = shifted + a1[r0:r0 + n_chunks] + pe_term
            outs.append(_dot(_gelu_tanh(pre).astype(BF16), w2_ref[kind]) + b2_ref[kind])
        carry_ref[kind] = a0
        out_ref[...] = jnp.concatenate(outs, axis=1).astype(out_ref.dtype)


def _compress_prompt(z3, cw, n_pages):
    bsz, t_len, cols = z3.shape
    z4 = z3.reshape(bsz, t_len, cols // HEAD_DIM, HEAD_DIM)
    blk = COL_KV // (2 * KV_WIDTH)
    specs = [pl.BlockSpec((None, PAGE, 2 * N_KV, HEAD_DIM),
                          functools.partial(lambda b, g, i: (b, g * n_pages + i, blk, 0), i=i))
             for i in range(n_pages)]
    return _compress(z4, specs, bsz, (t_len // PAGE) // n_pages, n_pages, cw)


def _compress_sample(cache_pages, layer, page_table, cw, n_pages):
    depth, n_pool = cache_pages.shape[:2]
    bsz, tot_pages = page_table.shape
    c5 = cache_pages.reshape(depth, n_pool, PAGE, 4 * N_KV, HEAD_DIM)
    specs = [pl.BlockSpec((None, None, PAGE, 2 * N_KV, HEAD_DIM),
                          functools.partial(lambda b, g, pt, i: (layer, pt[b, g * n_pages + i], 0, 0, 0), i=i))
             for i in range(n_pages)]
    return _compress(c5, specs, bsz, tot_pages // n_pages, n_pages, cw, prefetch=(page_table,))


def _compress(src, page_specs, n_batch, n_groups, n_pages, cw, prefetch=()):
    pef, w1, w2, b2 = cw
    n_chunks = n_pages * CHUNKS_PER_PAGE

    def const_spec(x):
        return pl.BlockSpec(x.shape, lambda b, g, *pt: (0,) * x.ndim)

    out_spec = pl.BlockSpec((None, n_chunks, KV_WIDTH), lambda b, g, *pt: (b, g, 0))
    out_shape = jax.ShapeDtypeStruct((n_batch, n_groups * n_chunks, KV_WIDTH), BF16)
    grid_spec = pltpu.PrefetchScalarGridSpec(
        num_scalar_prefetch=len(prefetch),
        grid=(n_batch, n_groups),
        in_specs=list(page_specs) + [const_spec(pef), const_spec(w1), const_spec(w2), const_spec(b2)],
        out_specs=(out_spec, out_spec),
        scratch_shapes=[pltpu.VMEM((2, N_KV * n_chunks, CMP_HIDDEN), F32)],
    )
    return pl.pallas_call(
        functools.partial(_compress_kernel, n_pages=n_pages, n_prefetch=len(prefetch)),
        out_shape=(out_shape, out_shape),
        grid_spec=grid_spec,
        compiler_params=_cparams(("parallel", "arbitrary")),
        name="compress",
    )(*prefetch, *([src] * n_pages), pef, w1, w2, b2)


def _flash_init(m_sc, l_sc, acc_sc, br):
    m_sc[br] = jnp.full(m_sc.shape[1:], -jnp.inf, F32)
    l_sc[br] = jnp.zeros(l_sc.shape[1:], F32)
    acc_sc[br] = jnp.zeros(acc_sc.shape[1:], F32)


def _flash_update(m_sc, l_sc, acc_sc, br, r0, nr, s, v):
    m_old = m_sc[br, r0:r0 + nr, :]
    m_new = jnp.maximum(m_old, jnp.max(s, axis=1, keepdims=True))
    alpha = jnp.exp(m_old - m_new)
    p = jnp.exp(s - (m_new if s.shape[1] == LANES else m_new[:, 0:1]))
    l_sc[br, r0:r0 + nr, :] = alpha * l_sc[br, r0:r0 + nr, :] + jnp.sum(p, axis=1, keepdims=True)
    acc_sc[br, r0:r0 + nr, :] = alpha * acc_sc[br, r0:r0 + nr, :] + _dot(p.astype(BF16), v)
    m_sc[br, r0:r0 + nr, :] = m_new


def _overlap_matrix(shape, n_slc):
    m = lax.broadcasted_iota(I32, shape, 0)
    j = lax.broadcasted_iota(I32, shape, 1)
    start = (m - 1) * CMP_STRIDE
    ov = (m >= 1) & (start < j * SLC_LEN + SLC_LEN) & (start + CMP_LEN > j * SLC_LEN) & (j < n_slc)
    return jnp.where(ov, 1.0, 0.0).astype(BF16)


def _attn_prompt_kernel(q_ref, kc_ref, vc_ref, ks_ref, vs_ref, kw_ref, vw_ref, g_ref, o_ref,
                        m_sc, l_sc, acc_sc, *, n_slc):
    qt = pl.program_id(2)
    tq = PAGE
    q = q_ref[...] * SCALE
    qs = [q[:, g * HEAD_DIM:(g + 1) * HEAD_DIM].astype(BF16) for g in range(GROUP)]
    row = lax.broadcasted_iota(I32, (tq, LANES), 0)
    lane = lax.broadcasted_iota(I32, (tq, LANES), 1)
    pos = qt * tq + row

    kc = kc_ref[...]
    vc = vc_ref[...]
    valid = (lane >= 1) & (lane * CMP_STRIDE + (CMP_LEN - CMP_STRIDE - 1) <= pos)
    vbias = jnp.where(valid, 0.0, NEG)
    vf = jnp.where(valid, 1.0, 0.0)
    o_cmp = []
    psum = jnp.zeros((tq, LANES), F32)
    for g in range(GROUP):
        s = _dot_nt(qs[g], kc) + vbias
        e = jnp.exp(s - jnp.max(s, axis=1, keepdims=True)) * vf
        denom = jnp.sum(e, axis=1, keepdims=True)
        p = e * (1.0 / jnp.maximum(denom, 1e-30))
        o_cmp.append(_dot(p.astype(BF16), vc))
        psum = psum + p

    imp = _split3_dot(psum, _overlap_matrix((LANES, LANES), n_slc))
    cur = pos >> SLC_SHIFT
    forced = (lane == 0) | (lane == cur) | (lane == cur - 1)
    score = jnp.where(lane <= cur, jnp.where(forced, FORCE, imp), -1.0)
    score = jnp.where(lane < n_slc, score, -2.0)
    rank = jnp.zeros((tq, LANES), F32)
    for jp in range(n_slc):
        col = score[:, jp:jp + 1]
        tie = jnp.where(lane > jp, 1.0, 0.0)
        rank = rank + jnp.where(col > score, 1.0, jnp.where(col == score, tie, 0.0))
    sel = jnp.where((rank < min(N_SEL, n_slc)) & (lane < n_slc), 1.0, 0.0).astype(BF16)

    _flash_init(m_sc, l_sc, acc_sc, 0)

    def slc_body(c, carry):
        k0 = pl.multiple_of(c * PAGE, PAGE)
        k = ks_ref[pl.ds(k0, PAGE), :].astype(BF16)
        v = vs_ref[pl.ds(k0, PAGE), :].astype(BF16)
        key = c * PAGE + lane
        expand = jnp.where((key >> SLC_SHIFT) == row, 1.0, 0.0).astype(BF16)
        selm = _dot(sel, expand)
        bias = jnp.where((selm > 0.5) & (key <= pos), 0.0, NEG)
        for g in range(GROUP):
            _flash_update(m_sc, l_sc, acc_sc, 0, g * tq, tq, _dot_nt(qs[g], k) + bias, v)
        return carry

    lax.fori_loop(0, qt + 1, slc_body, 0)

    _flash_init(m_sc, l_sc, acc_sc, 1)

    def win_body(c, carry):
        k0 = pl.multiple_of(c * PAGE, PAGE)
        k = kw_ref[pl.ds(k0, PAGE), :].astype(BF16)
        v = vw_ref[pl.ds(k0, PAGE), :].astype(BF16)
        diff = pos - (c * PAGE + lane)
        bias = jnp.where((diff >= 0) & (diff < WINDOW), 0.0, NEG)
        for g in range(GROUP):
            _flash_update(m_sc, l_sc, acc_sc, 1, g * tq, tq, _dot_nt(qs[g], k) + bias, v)
        return carry

    lax.fori_loop(jnp.maximum(qt - WINDOW // PAGE, 0), qt + 1, win_body, 0)

    gates = g_ref[...]
    for g in range(GROUP):
        r0 = g * tq
        o_slc = acc_sc[0, r0:r0 + tq, :] * (1.0 / l_sc[0, r0:r0 + tq, :])
        o_win = acc_sc[1, r0:r0 + tq, :] * (1.0 / l_sc[1, r0:r0 + tq, :])
        o = (gates[:, g:g + 1] * o_cmp[g] + gates[:, GROUP + g:GROUP + g + 1] * o_slc
             + gates[:, 2 * GROUP + g:2 * GROUP + g + 1] * o_win)
        o_ref[:, g * HEAD_DIM:(g + 1) * HEAD_DIM] = o.astype(o_ref.dtype)


def _attn_prompt(z, gates, kc, vc):
    bsz, t_len, _ = z.shape
    n_slc = -(-t_len // SLC_LEN)
    assert t_len % PAGE == 0 and n_slc <= LANES and kc.shape[1] <= LANES
    n_cmp_rows = kc.shape[1]
    assert n_cmp_rows == LANES, "compressed keys are laid out on one 128-lane tile"
    kv0 = COL_KV // HEAD_DIM

    def kv_spec(kind):
        return pl.BlockSpec((None, t_len, HEAD_DIM), lambda b, k, t: (b, 0, kv0 + kind * N_KV + k))

    cmp_spec = pl.BlockSpec((None, n_cmp_rows, HEAD_DIM), lambda b, k, t: (b, 0, k))
    q_blk0 = COL_Q // (GROUP * HEAD_DIM)
    return pl.pallas_call(
        functools.partial(_attn_prompt_kernel, n_slc=n_slc),
        out_shape=jax.ShapeDtypeStruct((bsz, t_len, ATTN_WIDTH), BF16),
        grid=(bsz, N_KV, t_len // PAGE),
        in_specs=[pl.BlockSpec((None, PAGE, GROUP * HEAD_DIM), lambda b, k, t: (b, t, q_blk0 + k)),
                  cmp_spec, cmp_spec,
                  kv_spec(2), kv_spec(3), kv_spec(4), kv_spec(5),
                  pl.BlockSpec((None, PAGE, LANES), lambda b, k, t: (b, t, k))],
        out_specs=pl.BlockSpec((None, PAGE, GROUP * HEAD_DIM), lambda b, k, t: (b, t, k)),
        scratch_shapes=[pltpu.VMEM((2, GROUP * PAGE, LANES), F32),
                        pltpu.VMEM((2, GROUP * PAGE, LANES), F32),
                        pltpu.VMEM((2, GROUP * PAGE, HEAD_DIM), F32)],
        compiler_params=_cparams(("parallel", "parallel", "arbitrary")),
        name="attn_prompt",
    )(z, kc, vc, z, z, z, z, gates)


def _attn_sample_kernel(*refs, n_pages, n_groups, t_new, n_slc, n_win):
    page_refs = refs[1:1 + n_pages]
    zs_ref, kc_ref, vc_ref, win_ref, g_ref, o_ref, ocmp_sc, sel_sc, m_sc, l_sc, acc_sc = refs[1 + n_pages:]
    grp = pl.program_id(1)
    tp = SUBLANES
    rows = GROUP * tp
    n_cmp_rows = kc_ref.shape[0]
    sel_lanes = sel_sc.shape[2]
    past_len = n_groups * n_pages * PAGE

    def q_heads(k):
        c0 = COL_Q + k * GROUP * HEAD_DIM
        q = zs_ref[:, c0:c0 + GROUP * HEAD_DIM] * SCALE
        return jnp.concatenate([q[:, g * HEAD_DIM:(g + 1) * HEAD_DIM] for g in range(GROUP)],
                               axis=0).astype(BF16)

    @pl.when(grp == 0)
    def _():
        ovb = _overlap_matrix((n_cmp_rows, sel_lanes), n_slc)
        mlane = lax.broadcasted_iota(I32, (rows, n_cmp_rows), 1)
        vbias = jnp.where(mlane >= 1, 0.0, NEG)
        lane = lax.broadcasted_iota(I32, (tp, sel_lanes), 1)
        lane_f = lane.astype(F32)
        tok = lax.broadcasted_iota(I32, (tp, sel_lanes), 0)
        cur = (past_len + tok) >> SLC_SHIFT
        forced = (lane == 0) | (lane == cur) | (lane == cur - 1)
        for k in range(N_KV):
            qk = q_heads(k)
            kc = kc_ref[:, k * HEAD_DIM:(k + 1) * HEAD_DIM]
            vc = vc_ref[:, k * HEAD_DIM:(k + 1) * HEAD_DIM]
            s = _dot_nt(qk, kc) + vbias
            e = jnp.exp(s - jnp.max(s, axis=1, keepdims=True))
            p = e * (1.0 / jnp.sum(e, axis=1, keepdims=True))
            ocmp_sc[k] = _dot(p.astype(BF16), vc)
            psum = p[0:tp]
            for g in range(1, GROUP):
                psum = psum + p[g * tp:(g + 1) * tp]
            imp = _split3_dot(psum, ovb)
            score = jnp.where(lane <= cur, jnp.where(forced, FORCE, imp), -1.0)
            score = jnp.where(lane < n_slc, score, -3e38)
            sel = jnp.zeros((tp, sel_lanes), F32)
            for _ in range(min(N_SEL, n_slc)):
                top = jnp.max(score, axis=1, keepdims=True)
                idx = jnp.min(jnp.where(score == top, lane_f, float(sel_lanes)), axis=1, keepdims=True)
                hit = lane_f == idx
                sel = jnp.where(hit, 1.0, sel)
                score = jnp.where(hit, -3e38, score)
            sel_sc[k] = sel
            _flash_init(m_sc, l_sc, acc_sc, k)

    nk = n_pages * PAGE
    jrow = lax.broadcasted_iota(I32, (sel_lanes, nk), 0)
    klane = lax.broadcasted_iota(I32, (sel_lanes, nk), 1)
    expand = jnp.where(jrow == grp * (nk // SLC_LEN) + (klane >> SLC_SHIFT), 1.0, 0.0).astype(BF16)
    for k in range(N_KV):
        kk = jnp.concatenate([p[:, k * HEAD_DIM:(k + 1) * HEAD_DIM].astype(BF16) for p in page_refs], axis=0)
        vv = jnp.concatenate([p[:, KV_WIDTH + k * HEAD_DIM:KV_WIDTH + (k + 1) * HEAD_DIM].astype(BF16)
                              for p in page_refs], axis=0)
        selm = _dot(sel_sc[k].astype(BF16), expand)
        bias = jnp.where(selm > 0.5, 0.0, NEG)
        bias = jnp.concatenate([bias] * GROUP, axis=0)
        _flash_update(m_sc, l_sc, acc_sc, k, 0, rows, _dot_nt(q_heads(k), kk) + bias, vv)

    @pl.when(grp == n_groups - 1)
    def _():
        tr = lax.broadcasted_iota(I32, (rows, tp), 0) % tp
        tc = lax.broadcasted_iota(I32, (rows, tp), 1)
        new_bias = jnp.where((tc <= tr) & (tc < t_new), 0.0, NEG)
        wr = lax.broadcasted_iota(I32, (rows, n_win), 0) % tp
        wc = lax.broadcasted_iota(I32, (rows, n_win), 1)
        wdiff = wr + n_win - wc
        win_bias = jnp.where((wdiff >= 0) & (wdiff < WINDOW), 0.0, NEG)
        gates = g_ref[...]
        for k in range(N_KV):
            qk = q_heads(k)

            def new_rows(kind, k=k):
                c0 = COL_KV + kind * KV_WIDTH + k * HEAD_DIM
                return zs_ref[:, c0:c0 + HEAD_DIM].astype(BF16)

            _flash_update(m_sc, l_sc, acc_sc, k, 0, rows, _dot_nt(qk, new_rows(2)) + new_bias, new_rows(3))
            o_slc = acc_sc[k] * (1.0 / l_sc[k])
            kw = win_ref[:, k * HEAD_DIM:(k + 1) * HEAD_DIM].astype(BF16)
            vw = win_ref[:, KV_WIDTH + k * HEAD_DIM:KV_WIDTH + (k + 1) * HEAD_DIM].astype(BF16)
            s1 = _dot_nt(qk, kw) + win_bias
            s2 = _dot_nt(qk, new_rows(4)) + new_bias
            mx = jnp.maximum(jnp.max(s1, axis=1, keepdims=True), jnp.max(s2, axis=1, keepdims=True))
            p1 = jnp.exp(s1 - mx)
            p2 = jnp.exp(s2 - mx)
            den = jnp.sum(p1, axis=1, keepdims=True) + jnp.sum(p2, axis=1, keepdims=True)
            o_win = (_dot(p1.astype(BF16), vw) + _dot(p2.astype(BF16), new_rows(5))) * (1.0 / den)
            o_cmp = ocmp_sc[k]
            for g in range(GROUP):
                r = slice(g * tp, (g + 1) * tp)
                c = k * LANES
                o = (gates[:, c + g:c + g + 1] * o_cmp[r]
                     + gates[:, c + GROUP + g:c + GROUP + g + 1] * o_slc[r]
                     + gates[:, c + 2 * GROUP + g:c + 2 * GROUP + g + 1] * o_win[r])
                h = k * GROUP + g
                o_ref[:, h * HEAD_DIM:(h + 1) * HEAD_DIM] = o.astype(o_ref.dtype)


def _attn_sample(cache_pages, layer, page_table, zs, gates, kc, vc, cache_win2, t_new, n_pages):
    bsz, tot_pages = page_table.shape
    n_groups = tot_pages // n_pages
    past_len = tot_pages * PAGE
    n_slc = -(-(past_len + t_new) // SLC_LEN)
    sel_lanes = -(-n_slc // LANES) * LANES
    n_win = cache_win2.shape[2]
    tp = SUBLANES
    rows = GROUP * tp
    page_specs = [pl.BlockSpec((None, None, PAGE, 2 * KV_WIDTH),
                               functools.partial(lambda b, g, pt, i: (layer, pt[b, g * n_pages + i], 0, 1), i=i))
                  for i in range(n_pages)]

    def bmap(b, g, pt):
        return (b, 0, 0)

    grid_spec = pltpu.PrefetchScalarGridSpec(
        num_scalar_prefetch=1,
        grid=(bsz, n_groups),
        in_specs=page_specs + [
            pl.BlockSpec((None, tp, zs.shape[2]), bmap),
            pl.BlockSpec((None, kc.shape[1], KV_WIDTH), bmap),
            pl.BlockSpec((None, vc.shape[1], KV_WIDTH), bmap),
            pl.BlockSpec((None, None, n_win, 2 * KV_WIDTH), lambda b, g, pt: (layer, b, 0, 0)),
            pl.BlockSpec((None, tp, N_KV * LANES), bmap)],
        out_specs=pl.BlockSpec((None, tp, ATTN_WIDTH), bmap),
        scratch_shapes=[pltpu.VMEM((N_KV, rows, HEAD_DIM), F32),
                        pltpu.VMEM((N_KV, tp, sel_lanes), F32),
                        pltpu.VMEM((N_KV, rows, LANES), F32),
                        pltpu.VMEM((N_KV, rows, LANES), F32),
                        pltpu.VMEM((N_KV, rows, HEAD_DIM), F32)],
    )
    return pl.pallas_call(
        functools.partial(_attn_sample_kernel, n_pages=n_pages, n_groups=n_groups, t_new=t_new,
                          n_slc=n_slc, n_win=n_win),
        out_shape=jax.ShapeDtypeStruct((bsz, tp, ATTN_WIDTH), BF16),
        grid_spec=grid_spec,
        compiler_params=_cparams(("parallel", "arbitrary")),
        name="attn_sample",
    )(page_table, *([cache_pages] * n_pages), zs, kc, vc, cache_win2, gates)


CONV_HALO = 32
CONV_LANE_CHUNK = 256


def _conv_kernel(*refs, tc, multi_tile):
    if multi_tile:
        u_ref, up_ref, buf_ref, dw_ref, db_ref, lg_ref, lb_ref, o_ref, tail_ref, ext_sc, y_sc = refs
    else:
        u_ref, buf_ref, dw_ref, db_ref, lg_ref, lb_ref, o_ref, tail_ref, ext_sc, y_sc = refs
    t = pl.program_id(1)

    def glu(u):
        return u[:, :CONV_CH] * jax.nn.sigmoid(u[:, CONV_CH:])

    @pl.when(t == 0)
    def _():
        ext_sc[0:CONV_HALO, :] = buf_ref[...]

    if multi_tile:
        @pl.when(t > 0)
        def _():
            ext_sc[0:CONV_HALO, :] = glu(up_ref[...])

    ext_sc[CONV_HALO:CONV_HALO + tc, :] = glu(u_ref[...])
    off = CONV_HALO - (CONV_WIDTH - 1)
    for c0 in range(0, CONV_CH, CONV_LANE_CHUNK):
        cs = slice(c0, c0 + CONV_LANE_CHUNK)
        acc = jnp.broadcast_to(db_ref[:, cs], (tc, CONV_LANE_CHUNK))
        for w in range(CONV_WIDTH):
            acc = acc + ext_sc[off + w:off + w + tc, cs] * dw_ref[w:w + 1, cs]
        y_sc[:, cs] = acc
    y = y_sc[...]
    mu = jnp.mean(y, axis=-1, keepdims=True)
    d = y - mu
    var = jnp.mean(d * d, axis=-1, keepdims=True)
    yn = d * lax.rsqrt(var + LN_EPS) * lg_ref[...] + lb_ref[...]
    o_ref[...] = (yn * jax.nn.sigmoid(yn)).astype(o_ref.dtype)

    @pl.when(t == pl.num_programs(1) - 1)
    def _():
        tail_ref[...] = ext_sc[tc:tc + CONV_HALO, :]


def _conv_group(z, buf, dw_w, dw_b, ln_g, ln_b, tc):
    bsz, t_len, _ = z.shape
    n_t = t_len // tc
    multi_tile = n_t > 1
    assert COL_U == 0 and (not multi_tile or tc % CONV_HALO == 0)
    halo_per_tile = max(tc // CONV_HALO, 1)

    def vec(v):
        return v.reshape(1, CONV_CH)

    cmap = lambda b, t: (0, 0)
    in_specs = [pl.BlockSpec((None, tc, 2 * CONV_CH), lambda b, t: (b, t, 0))]
    args = [z]
    if multi_tile:
        in_specs.append(pl.BlockSpec((None, CONV_HALO, 2 * CONV_CH),
                                     lambda b, t: (b, jnp.maximum(t * halo_per_tile - 1, 0), 0)))
        args.append(z)
    in_specs += [pl.BlockSpec((None, CONV_HALO, CONV_CH), lambda b, t: (b, 0, 0)),
                 pl.BlockSpec((CONV_WIDTH, CONV_CH), cmap),
                 pl.BlockSpec((1, CONV_CH), cmap), pl.BlockSpec((1, CONV_CH), cmap),
                 pl.BlockSpec((1, CONV_CH), cmap)]
    args += [buf, dw_w, vec(dw_b), vec(ln_g), vec(ln_b)]
    return pl.pallas_call(
        functools.partial(_conv_kernel, tc=tc, multi_tile=multi_tile),
        out_shape=(jax.ShapeDtypeStruct((bsz, t_len, CONV_CH), BF16),
                   jax.ShapeDtypeStruct((bsz, CONV_HALO, CONV_CH), F32)),
        grid=(bsz, n_t),
        in_specs=in_specs,
        out_specs=(pl.BlockSpec((None, tc, CONV_CH), lambda b, t: (b, t, 0)),
                   pl.BlockSpec((None, CONV_HALO, CONV_CH), lambda b, t: (b, 0, 0))),
        scratch_shapes=[pltpu.VMEM((CONV_HALO + tc, CONV_CH), F32), pltpu.VMEM((tc, CONV_CH), F32)],
        compiler_params=_cparams(("parallel", "arbitrary")),
        name="conv_group",
    )(*args)


def _rope_tables(pos):
    half = ROT_DIM // 2
    inv = ROPE_THETA ** (-jnp.arange(half, dtype=F32) / half)
    ang = pos.astype(F32)[:, None] * inv[None, :]
    cos, sin = jnp.cos(ang), jnp.sin(ang)
    n = pos.shape[0]
    zeros = jnp.zeros((n, half), F32)
    rest = HEAD_DIM - ROT_DIM
    c = jnp.concatenate([cos, cos, jnp.ones((n, rest), F32)], axis=1)
    s1 = jnp.concatenate([zeros, sin, jnp.zeros((n, rest), F32)], axis=1)
    s2 = jnp.concatenate([-sin, zeros, jnp.zeros((n, rest), F32)], axis=1)
    return c, s1, s2


def _prep_layer(w_in_l, w_out_l, w_gate_l, w_up_l, w_down_l, cmp_pe_l, cmp_w1_l, cmp_w2_l, cmp_b2_l, d_ff_pad):
    d = w_in_l.shape[0]
    kv_end = ATTN_WIDTH + KV_COLS
    gate_end = kv_end + 3 * N_HEADS
    w_main = jnp.concatenate([w_in_l[:, gate_end:], w_in_l[:, :kv_end]], axis=1).astype(BF16)
    wg = w_in_l[:, kv_end:gate_end].reshape(d, 3, N_KV, GROUP).transpose(0, 2, 1, 3).reshape(d, N_KV, 3 * GROUP)
    wg = jnp.pad(wg, ((0, 0), (0, 0), (0, LANES - 3 * GROUP))).reshape(d, N_KV * LANES).astype(BF16)
    padf = d_ff_pad - w_gate_l.shape[1]
    w_gate_b = jnp.pad(w_gate_l.astype(BF16), ((0, 0), (0, padf)))
    w_up_b = jnp.pad(w_up_l.astype(BF16), ((0, 0), (0, padf)))
    w_down_b = jnp.pad(w_down_l.astype(BF16), ((0, padf), (0, 0)))
    pef = jnp.pad(cmp_pe_l.reshape(2, 1, CMP_LEN * HEAD_DIM), ((0, 0), (0, SUBLANES - 1), (0, 0))).astype(BF16)
    w1 = cmp_w1_l.astype(BF16).reshape(2, CMP_LEN // CMP_STRIDE, CMP_STRIDE * HEAD_DIM, CMP_HIDDEN)
    cw = (pef, w1, cmp_w2_l.astype(BF16), cmp_b2_l.reshape(2, 1, HEAD_DIM))
    return w_main, wg, w_out_l.astype(BF16), w_gate_b, w_up_b, w_down_b, cw


def _dense_tail(x, attn, conv, w_out_b, norm_ffn_l, w_gate_b, w_up_b, w_down_b, tm):
    d = x.shape[1]
    h = _outproj(attn, conv, w_out_b, x, tm, 512)
    hn = _rmsnorm(h, norm_ffn_l, BF16)
    act = _ffn_up(hn, w_gate_b, w_up_b, tm, 512)
    return _ffn_down(act, w_down_b, h, tm, min(d, 1024), 1024)


def kernel(x_prompt, x_sample, cache_kv, cache_win, state_conv, page_table, norm_mix, w_in, cmp_pe, cmp_w1,
           cmp_w2, cmp_b2, conv_dw_w, conv_dw_b, conv_ln_g, conv_ln_b, w_out, norm_ffn, w_gate, w_up, w_down,
           norm_final):
    depth = w_in.shape[0]
    bsz, seq, d = x_prompt.shape
    dec_b, dec_t, _ = x_sample.shape
    n_pool, page_size = cache_kv.shape[1], cache_kv.shape[2]
    tot_pages = page_table.shape[1]
    past_len = tot_pages * page_size
    d_ff_pad = -(-w_gate.shape[2] // 1024) * 1024
    assert page_size == PAGE and seq % PAGE == 0 and dec_t <= SUBLANES
    assert conv_dw_w.shape[2] == CONV_CH and d == ATTN_WIDTH + CONV_CH
    pages_per_group = min(16, tot_pages, seq // PAGE)
    assert tot_pages % pages_per_group == 0 and (seq // PAGE) % pages_per_group == 0
    tp = SUBLANES
    ms = dec_b * tp

    tm_p = min(512, seq)
    tabs_p = _rope_tables(jnp.arange(seq, dtype=I32))
    tabs_s = tuple(jnp.tile(t, (dec_b, 1)) for t in _rope_tables(past_len + jnp.arange(tp, dtype=I32)))

    hp = x_prompt.reshape(bsz * seq, d)
    hs = jnp.pad(x_sample, ((0, 0), (0, tp - dec_t), (0, 0))).reshape(ms, d)
    cache_pages = cache_kv.reshape(depth, n_pool, page_size, 4 * KV_WIDTH)
    cache_win2 = cache_win.reshape(depth, dec_b, cache_win.shape[2], 2 * KV_WIDTH)
    zero_buf = jnp.zeros((bsz, CONV_HALO, CONV_CH), F32)
    tm_dense = min(1024, bsz * seq)

    kvp, winp, convp, kvs, wins, convs = [], [], [], [], [], []
    for l in range(depth):
        w_main, wg, w_out_b, w_gate_b, w_up_b, w_down_b, cw = _prep_layer(
            w_in[l], w_out[l], w_gate[l], w_up[l], w_down[l], cmp_pe[l], cmp_w1[l], cmp_w2[l], cmp_b2[l],
            d_ff_pad)

        xn = _rmsnorm(hp, norm_mix[l], BF16)
        z, gates = _inproj(xn, w_main, wg, tabs_p, seq // tm_p, tm_p)
        z3 = z.reshape(bsz, seq, z.shape[1])
        kc, vc = _compress_prompt(z3, cw, pages_per_group)
        attn = _attn_prompt(z3, gates.reshape(bsz, seq, N_KV * LANES), kc, vc)
        conv, tail = _conv_group(z3, zero_buf, conv_dw_w[l], conv_dw_b[l], conv_ln_g[l], conv_ln_b[l], PAGE)
        hp = _dense_tail(hp, attn.reshape(bsz * seq, ATTN_WIDTH), conv.reshape(bsz * seq, CONV_CH), w_out_b,
                         norm_ffn[l], w_gate_b, w_up_b, w_down_b, tm_dense)
        kvp.append(z3[:, :, COL_KV:COL_KV + 4 * KV_WIDTH].reshape(bsz, seq, 4, N_KV, HEAD_DIM))
        keep = min(WINDOW, seq)
        winp.append(z3[:, seq - keep:, COL_KV + 4 * KV_WIDTH:COL_KV + KV_COLS].reshape(bsz, keep, 2, N_KV, HEAD_DIM))
        convp.append(tail[:, CONV_HALO - (CONV_WIDTH - 1):])

        xn_s = _rmsnorm(hs, norm_mix[l], BF16)
        zs, gates_s = _inproj(xn_s, w_main, wg, tabs_s, 1, ms)
        zs3 = zs.reshape(dec_b, tp, zs.shape[1])
        kc_s, vc_s = _compress_sample(cache_pages, l, page_table, cw, pages_per_group)
        attn_s = _attn_sample(cache_pages, l, page_table, zs3, gates_s.reshape(dec_b, tp, N_KV * LANES),
                              kc_s, vc_s, cache_win2, dec_t, pages_per_group)
        buf_s = jnp.pad(state_conv[l], ((0, 0), (CONV_HALO - (CONV_WIDTH - 1), 0), (0, 0)))
        conv_s, tail_s = _conv_group(zs3, buf_s, conv_dw_w[l], conv_dw_b[l], conv_ln_g[l], conv_ln_b[l], tp)
        hs = _dense_tail(hs, attn_s.reshape(ms, ATTN_WIDTH), conv_s.reshape(ms, CONV_CH), w_out_b, norm_ffn[l],
                         w_gate_b, w_up_b, w_down_b, ms)
        kvs.append(zs3[:, :dec_t, COL_KV:COL_KV + 4 * KV_WIDTH].reshape(dec_b, dec_t, 4, N_KV, HEAD_DIM))
        new_win_s = zs3[:, :dec_t, COL_KV + 4 * KV_WIDTH:COL_KV + KV_COLS].reshape(dec_b, dec_t, 2, N_KV, HEAD_DIM)
        win_all = jnp.concatenate([cache_win[l], new_win_s], axis=1)
        wins.append(win_all[:, win_all.shape[1] - min(WINDOW, win_all.shape[1]):])
        new_glu = tail_s[:, CONV_HALO - tp:CONV_HALO - tp + dec_t]
        ext_s = jnp.concatenate([state_conv[l], new_glu], axis=1)
        convs.append(ext_s[:, ext_s.shape[1] - (CONV_WIDTH - 1):])

    y_prompt = _rmsnorm(hp, norm_final, F32).reshape(bsz, seq, d)
    y_sample = _rmsnorm(hs, norm_final, F32).reshape(dec_b, tp, d)[:, :dec_t]
    return (y_prompt, y_sample, jnp.stack(kvp), jnp.stack(winp), jnp.stack(convp),
            jnp.stack(kvs), jnp.stack(wins), jnp.stack(convs))
```

```python
import functools
import math

import jax
import jax.numpy as jnp
from jax import lax
from jax.experimental import pallas as pl
from jax.experimental.pallas import tpu as pltpu

F32 = jnp.float32
BF16 = jnp.bfloat16
I32 = jnp.int32

N_HEADS = 16
N_KV = 4
HEAD_DIM = 128
GROUP = N_HEADS // N_KV
ATTN_WIDTH = N_HEADS * HEAD_DIM
KV_WIDTH = N_KV * HEAD_DIM
CONV_CH = 2048
CONV_WIDTH = 31
ROT_DIM = HEAD_DIM // 4
ROPE_THETA = 500000.0
CMP_LEN = 32
CMP_STRIDE = 16
CMP_HIDDEN = 2 * HEAD_DIM
SLC_LEN = 64
SLC_SHIFT = 6
N_SEL = 16
WINDOW = 512
SCALE = HEAD_DIM ** -0.5
LOG2E = 1.0 / math.log(2.0)
SLC_KEY_BLOCK = 512
NEG = -1e30
FORCE = 1e9
RMS_EPS = 1e-6
LN_EPS = 1e-5

LANES = 128
SUBLANES = 8
VMEM_LIMIT = 56 * 1024 * 1024

PAGE = 128
CHUNKS_PER_PAGE = PAGE // CMP_STRIDE
COL_U = 0
COL_Q = 2 * CONV_CH
COL_KV = COL_Q + ATTN_WIDTH
KV_COLS = 6 * KV_WIDTH
PROJ_TN = 512


def _cparams(sem):
    return pltpu.CompilerParams(dimension_semantics=sem, vmem_limit_bytes=VMEM_LIMIT)


def _dot(a, b):
    return jnp.dot(a, b, preferred_element_type=F32)


def _dot_nt(a, b):
    return lax.dot_general(a, b, (((1,), (1,)), ((), ())), preferred_element_type=F32)


def _split3_dot(x, m_bf16):
    hi = x.astype(BF16)
    r1 = x - hi.astype(F32)
    mid = r1.astype(BF16)
    lo = (r1 - mid.astype(F32)).astype(BF16)
    return _dot(hi, m_bf16) + _dot(mid, m_bf16) + _dot(lo, m_bf16)


def _rmsnorm_kernel(x_ref, g_ref, o_ref):
    x = x_ref[...]
    ms = jnp.mean(x * x, axis=-1, keepdims=True)
    o_ref[...] = (x * lax.rsqrt(ms + RMS_EPS) * g_ref[...]).astype(o_ref.dtype)


def _rmsnorm(x, g, out_dtype):
    m, d = x.shape
    tm = min(m, 256)
    return pl.pallas_call(
        _rmsnorm_kernel,
        out_shape=jax.ShapeDtypeStruct((m, d), out_dtype),
        grid=(m // tm,),
        in_specs=[pl.BlockSpec((tm, d), lambda i: (i, 0)),
                  pl.BlockSpec((1, d), lambda i: (0, 0))],
        out_specs=pl.BlockSpec((tm, d), lambda i: (i, 0)),
        compiler_params=_cparams(("parallel",)),
        name="rmsnorm",
    )(x, g.reshape(1, d))


def _inproj_kernel(a_ref, w_ref, wg_ref, c_ref, s1_ref, s2_ref, z_ref, g_ref):
    j = pl.program_id(1)
    a = a_ref[...]
    acc = _dot(a, w_ref[...])
    q0 = COL_Q // PROJ_TN
    k0 = COL_KV // PROJ_TN
    is_rope = ((j >= q0) & (j <= k0)) | (j == k0 + 2) | (j == k0 + 4)

    @pl.when(is_rope)
    def _():
        c = c_ref[...]
        s1 = s1_ref[...]
        s2 = s2_ref[...]
        for h in range(PROJ_TN // HEAD_DIM):
            x = acc[:, h * HEAD_DIM:(h + 1) * HEAD_DIM]
            z_ref[:, h * HEAD_DIM:(h + 1) * HEAD_DIM] = (
                x * c + pltpu.roll(x, ROT_DIM // 2, 1) * s1
                + pltpu.roll(x, HEAD_DIM - ROT_DIM // 2, 1) * s2)

    @pl.when(jnp.logical_not(is_rope))
    def _():
        z_ref[...] = acc

    @pl.when(j == 0)
    def _():
        g_ref[...] = jax.nn.sigmoid(_dot(a, wg_ref[...]))


def _inproj(xn, w, wg, tabs, tab_tiles, tm):
    m, k = xn.shape
    n = w.shape[1]
    c, s1, s2 = tabs
    tab_spec = pl.BlockSpec((tm, HEAD_DIM), lambda i, j: (i % tab_tiles, 0))
    return pl.pallas_call(
        _inproj_kernel,
        out_shape=(jax.ShapeDtypeStruct((m, n), F32),
                   jax.ShapeDtypeStruct((m, N_KV * LANES), F32)),
        grid=(m // tm, n // PROJ_TN),
        in_specs=[pl.BlockSpec((tm, k), lambda i, j: (i, 0)),
                  pl.BlockSpec((k, PROJ_TN), lambda i, j: (0, j)),
                  pl.BlockSpec((k, N_KV * LANES), lambda i, j: (0, 0)),
                  tab_spec, tab_spec, tab_spec],
        out_specs=(pl.BlockSpec((tm, PROJ_TN), lambda i, j: (i, j)),
                   pl.BlockSpec((tm, N_KV * LANES), lambda i, j: (i, 0))),
        compiler_params=_cparams(("parallel", "arbitrary")),
        name="inproj",
    )(xn, w, wg, c, s1, s2)


def _outproj_kernel(a1_ref, a2_ref, w1_ref, w2_ref, r_ref, o_ref):
    o_ref[...] = r_ref[...] + _dot(a1_ref[...], w1_ref[...]) + _dot(a2_ref[...], w2_ref[...])


def _outproj(a1, a2, w, res, tm, tn):
    m, k1 = a1.shape
    k2 = a2.shape[1]
    n = w.shape[1]
    assert k1 == k2 and w.shape[0] == k1 + k2
    return pl.pallas_call(
        _outproj_kernel,
        out_shape=jax.ShapeDtypeStruct((m, n), F32),
        grid=(m // tm, n // tn),
        in_specs=[pl.BlockSpec((tm, k1), lambda i, j: (i, 0)),
                  pl.BlockSpec((tm, k2), lambda i, j: (i, 0)),
                  pl.BlockSpec((k1, tn), lambda i, j: (0, j)),
                  pl.BlockSpec((k2, tn), lambda i, j: (1, j)),
                  pl.BlockSpec((tm, tn), lambda i, j: (i, j))],
        out_specs=pl.BlockSpec((tm, tn), lambda i, j: (i, j)),
        compiler_params=_cparams(("parallel", "arbitrary")),
        name="outproj",
    )(a1, a2, w, w, res)


def _ffn_up_kernel(a_ref, wg_ref, wu_ref, o_ref):
    a = a_ref[...]
    gate = _dot(a, wg_ref[...])
    up = _dot(a, wu_ref[...])
    o_ref[...] = (gate * jax.nn.sigmoid(gate) * up).astype(o_ref.dtype)


def _ffn_up(a, wg, wu, tm, tn):
    m, k = a.shape
    n = wg.shape[1]
    return pl.pallas_call(
        _ffn_up_kernel,
        out_shape=jax.ShapeDtypeStruct((m, n), BF16),
        grid=(m // tm, n // tn),
        in_specs=[pl.BlockSpec((tm, k), lambda i, j: (i, 0)),
                  pl.BlockSpec((k, tn), lambda i, j: (0, j)),
                  pl.BlockSpec((k, tn), lambda i, j: (0, j))],
        out_specs=pl.BlockSpec((tm, tn), lambda i, j: (i, j)),
        compiler_params=_cparams(("parallel", "arbitrary")),
        name="ffn_up",
    )(a, wg, wu)


def _ffn_down_kernel(a_ref, w_ref, r_ref, o_ref, acc_ref):
    kk = pl.program_id(2)

    @pl.when(kk == 0)
    def _():
        acc_ref[...] = r_ref[...]

    acc_ref[...] += _dot(a_ref[...], w_ref[...])

    @pl.when(kk == pl.num_programs(2) - 1)
    def _():
        o_ref[...] = acc_ref[...]


def _ffn_down(a, w, res, tm, tn, tk):
    m, k = a.shape
    n = w.shape[1]
    return pl.pallas_call(
        _ffn_down_kernel,
        out_shape=jax.ShapeDtypeStruct((m, n), F32),
        grid=(m // tm, n // tn, k // tk),
        in_specs=[pl.BlockSpec((tm, tk), lambda i, j, kk: (i, kk)),
                  pl.BlockSpec((tk, tn), lambda i, j, kk: (kk, j)),
                  pl.BlockSpec((tm, tn), lambda i, j, kk: (i, j))],
        out_specs=pl.BlockSpec((tm, tn), lambda i, j, kk: (i, j)),
        scratch_shapes=[pltpu.VMEM((tm, tn), F32)],
        compiler_params=_cparams(("parallel", "parallel", "arbitrary")),
        name="ffn_down",
    )(a, w, res)


def _gelu_tanh(x):
    return 0.5 * x * (1.0 + jnp.tanh(math.sqrt(2.0 / math.pi) * (x + 0.044715 * (x * x * x))))


def _compress_kernel(*refs, n_pages, n_prefetch):
    page_refs = refs[n_prefetch:n_prefetch + n_pages]
    pef_ref, w1_ref, w2_ref, b2_ref, kc_ref, vc_ref, carry_ref = refs[n_prefetch + n_pages:]
    grp = pl.program_id(1)
    n_chunks = n_pages * CHUNKS_PER_PAGE
    half_k = CMP_STRIDE * HEAD_DIM

    @pl.when(grp == 0)
    def _():
        carry_ref[...] = jnp.zeros_like(carry_ref)

    first = lax.broadcasted_iota(I32, (n_chunks, CMP_HIDDEN), 0) == 0
    pages = [p.reshape(PAGE * 2 * N_KV, HEAD_DIM) for p in page_refs]
    for kind, out_ref in ((0, kc_ref), (1, vc_ref)):
        cols = []
        for j in range(CMP_STRIDE):
            cols.append(jnp.concatenate(
                [p[pl.ds(j * 2 * N_KV + kind * N_KV + h, CHUNKS_PER_PAGE, stride=CMP_STRIDE * 2 * N_KV), :]
                 for h in range(N_KV) for p in pages], axis=0).astype(BF16))
        x_all = jnp.concatenate(cols, axis=1)
        a0 = _dot(x_all, w1_ref[kind, 0])
        a1 = _dot(x_all, w1_ref[kind, 1])
        pe = pef_ref[kind]
        pe_term = (_dot(pe[:, :half_k], w1_ref[kind, 0]) + _dot(pe[:, half_k:], w1_ref[kind, 1]))[0:1, :]
        outs = []
        for h in range(N_KV):
            r0 = h * n_chunks
            prev_last = carry_ref[kind, r0 + n_chunks - 1:r0 + n_chunks, :]
            shifted = jnp.where(first, prev_last, pltpu.roll(a0[r0:r0 + n_chunks], 1, 0))
            pre = shifted + a1[r0:r0 + n_chunks] + pe_term
            outs.append(_dot(_gelu_tanh(pre).astype(BF16), w2_ref[kind]) + b2_ref[kind])
        carry_ref[kind] = a0
        out_ref[...] = jnp.concatenate(outs, axis=1).astype(out_ref.dtype)


def _compress_prompt(z3, cw, n_pages):
    bsz, t_len, cols = z3.shape
    z4 = z3.reshape(bsz, t_len, cols // HEAD_DIM, HEAD_DIM)
    blk = COL_KV // (2 * KV_WIDTH)
    specs = [pl.BlockSpec((None, PAGE, 2 * N_KV, HEAD_DIM),
                          functools.partial(lambda b, g, i: (b, g * n_pages + i, blk, 0), i=i))
             for i in range(n_pages)]
    return _compress(z4, specs, bsz, (t_len // PAGE) // n_pages, n_pages, cw)


def _compress_sample(c5, layer, page_table, cw, n_pages):
    bsz, tot_pages = page_table.shape
    specs = [pl.BlockSpec((None, None, PAGE, 2 * N_KV, HEAD_DIM),
                          functools.partial(lambda b, g, pt, i: (layer, pt[b, g * n_pages + i], 0, 0, 0), i=i))
             for i in range(n_pages)]
    return _compress(c5, specs, bsz, tot_pages // n_pages, n_pages, cw, prefetch=(page_table,))


def _compress(src, page_specs, n_batch, n_groups, n_pages, cw, prefetch=()):
    pef, w1, w2, b2 = cw
    n_chunks = n_pages * CHUNKS_PER_PAGE

    def const_spec(x):
        return pl.BlockSpec(x.shape, lambda b, g, *pt: (0,) * x.ndim)

    out_spec = pl.BlockSpec((None, n_chunks, KV_WIDTH), lambda b, g, *pt: (b, g, 0))
    out_shape = jax.ShapeDtypeStruct((n_batch, n_groups * n_chunks, KV_WIDTH), BF16)
    grid_spec = pltpu.PrefetchScalarGridSpec(
        num_scalar_prefetch=len(prefetch),
        grid=(n_batch, n_groups),
        in_specs=list(page_specs) + [const_spec(pef), const_spec(w1), const_spec(w2), const_spec(b2)],
        out_specs=(out_spec, out_spec),
        scratch_shapes=[pltpu.VMEM((2, N_KV * n_chunks, CMP_HIDDEN), F32)],
    )
    return pl.pallas_call(
        functools.partial(_compress_kernel, n_pages=n_pages, n_prefetch=len(prefetch)),
        out_shape=(out_shape, out_shape),
        grid_spec=grid_spec,
        compiler_params=_cparams(("parallel", "arbitrary")),
        name="compress",
    )(*prefetch, *([src] * n_pages), pef, w1, w2, b2)


def _flash_init(m_sc, l_sc, acc_sc, br):
    m_sc[br] = jnp.full(m_sc.shape[1:], -jnp.inf, F32)
    l_sc[br] = jnp.zeros(l_sc.shape[1:], F32)
    acc_sc[br] = jnp.zeros(acc_sc.shape[1:], F32)


def _flash_update(m_sc, l_sc, acc_sc, br, r0, nr, s, v):
    m_old = m_sc[br, r0:r0 + nr, :]
    m_new = jnp.maximum(m_old, jnp.max(s, axis=1, keepdims=True))
    alpha = jnp.exp(m_old - m_new)
    p = jnp.exp(s - (m_new if s.shape[1] == LANES else m_new[:, 0:1]))
    l_sc[br, r0:r0 + nr, :] = alpha * l_sc[br, r0:r0 + nr, :] + jnp.sum(p, axis=1, keepdims=True)
    acc_sc[br, r0:r0 + nr, :] = alpha * acc_sc[br, r0:r0 + nr, :] + _dot(p.astype(BF16), v)
    m_sc[br, r0:r0 + nr, :] = m_new


def _overlap_matrix(shape, n_slc):
    m = lax.broadcasted_iota(I32, shape, 0)
    j = lax.broadcasted_iota(I32, shape, 1)
    start = (m - 1) * CMP_STRIDE
    ov = (m >= 1) & (start < j * SLC_LEN + SLC_LEN) & (start + CMP_LEN > j * SLC_LEN) & (j < n_slc)
    return jnp.where(ov, 1.0, 0.0).astype(BF16)


def _lane_tile_reduce(x, op):
    out = x[..., 0:LANES]
    for i in range(1, x.shape[-1] // LANES):
        out = op(out, x[..., i * LANES:(i + 1) * LANES])
    return out


def _attn_prompt_kernel(q_ref, kc_ref, vc_ref, ks_ref, vs_ref, kw_ref, vw_ref, g_ref, o_ref,
                        ksb_sc, vsb_sc, kwb_sc, vwb_sc, s_sc, m_sc, l_sc, acc_sc, *, n_slc, kb, win_keys):
    qt = pl.program_id(2)
    tq = PAGE
    rows = GROUP * tq
    t_len = ks_ref.shape[0]

    @pl.when(qt == 0)
    def _():
        ksb_sc[...] = ks_ref[...].astype(BF16)
        vsb_sc[...] = vs_ref[...].astype(BF16)
        kwb_sc[...] = kw_ref[...].astype(BF16)
        vwb_sc[...] = vw_ref[...].astype(BF16)

    q = q_ref[...] * (SCALE * LOG2E)
    q_all = jnp.concatenate([q[:, g * HEAD_DIM:(g + 1) * HEAD_DIM] for g in range(GROUP)],
                            axis=0).astype(BF16)
    row = lax.broadcasted_iota(I32, (tq, LANES), 0)
    lane = lax.broadcasted_iota(I32, (tq, LANES), 1)
    pos = qt * tq + row

    valid = (lane >= 1) & (lane * CMP_STRIDE + (CMP_LEN - CMP_STRIDE - 1) <= pos)
    vbias = jnp.where(valid, 0.0, NEG)
    vf = jnp.where(valid, 1.0, 0.0)
    s3 = _dot_nt(q_all, kc_ref[...]).reshape(GROUP, tq, LANES) + vbias[None]
    e = jnp.exp2(s3 - jnp.max(s3, axis=2, keepdims=True)) * vf[None]
    denom = jnp.sum(e, axis=2, keepdims=True)
    p3 = e * (1.0 / jnp.maximum(denom, 1e-30))
    o_cmp = _dot(p3.reshape(rows, LANES).astype(BF16), vc_ref[...])
    psum = p3[0] + p3[1] + p3[2] + p3[3]

    imp = _split3_dot(psum, _overlap_matrix((LANES, LANES), n_slc))
    cur = pos >> SLC_SHIFT
    forced = (lane == 0) | (lane == cur) | (lane == cur - 1)
    score = jnp.where(lane <= cur, jnp.where(forced, FORCE, imp), -1.0)
    score = jnp.where(lane < n_slc, score, -2.0)
    n_rank = -(-n_slc // SUBLANES) * SUBLANES
    score_t = score.T[0:n_rank]
    jrow = lax.broadcasted_iota(I32, (n_rank, tq), 0)
    rank_t = jnp.zeros((n_rank, tq), F32)
    for jp in range(n_slc):
        other = score_t[jp:jp + 1, :]
        tie = jnp.where(jrow > jp, 1.0, 0.0)
        rank_t = rank_t + jnp.where(other > score_t, 1.0, jnp.where(other == score_t, tie, 0.0))
    sel_t = jnp.where((rank_t < min(N_SEL, n_slc)) & (jrow < n_slc), 1.0, 0.0)
    sel = jnp.concatenate([sel_t, jnp.zeros((LANES - n_rank, tq), F32)], axis=0).T.astype(BF16)

    m_sc[...] = jnp.full(m_sc.shape, -jnp.inf, F32)
    l_sc[...] = jnp.zeros(l_sc.shape, F32)
    acc_sc[...] = jnp.zeros(acc_sc.shape, F32)
    n_kb = t_len // kb
    brow = lax.broadcasted_iota(I32, (LANES, kb), 0)
    blane = lax.broadcasted_iota(I32, (tq, kb), 1)
    bpos = qt * tq + lax.broadcasted_iota(I32, (tq, kb), 0)
    for cb in range(n_kb):
        @pl.when(cb * kb <= qt * tq)
        def _(cb=cb):
            key = cb * kb + blane
            expand = jnp.where(((cb * kb + lax.broadcasted_iota(I32, (LANES, kb), 1)) >> SLC_SHIFT) == brow,
                               1.0, 0.0).astype(BF16)
            selm = _dot(sel, expand)
            bias = jnp.where((selm > 0.5) & (key <= bpos), 0.0, NEG)
            s = _dot_nt(q_all, ksb_sc[cb * kb:(cb + 1) * kb, :]).reshape(GROUP, tq, kb) + bias[None]
            s_sc[:, cb * kb:(cb + 1) * kb] = s.reshape(rows, kb)
            m_sc[...] = jnp.maximum(m_sc[...], _lane_tile_reduce(s, jnp.maximum).reshape(rows, LANES))

    m_row = jnp.max(m_sc[...], axis=1, keepdims=True)
    for cb in range(n_kb):
        @pl.when(cb * kb <= qt * tq)
        def _(cb=cb):
            p = jnp.exp2(s_sc[:, cb * kb:(cb + 1) * kb] - m_row)
            l_sc[...] += _lane_tile_reduce(p, jnp.add)
            acc_sc[...] += _dot(p.astype(BF16), vsb_sc[cb * kb:(cb + 1) * kb, :])

    o_slc = acc_sc[...] * (1.0 / jnp.sum(l_sc[...], axis=1, keepdims=True))

    start = pl.multiple_of(jnp.maximum(qt * tq + tq - win_keys, 0), PAGE)
    wlane = lax.broadcasted_iota(I32, (tq, win_keys), 1)
    wdiff = (qt * tq + lax.broadcasted_iota(I32, (tq, win_keys), 0)) - (start + wlane)
    wbias = jnp.where((wdiff >= 0) & (wdiff < WINDOW), 0.0, NEG)
    sw = _dot_nt(q_all, kwb_sc[pl.ds(start, win_keys), :]).reshape(GROUP, tq, win_keys) + wbias[None]
    pw = jnp.exp2(sw - jnp.max(sw, axis=2, keepdims=True))
    lw = jnp.sum(pw, axis=2, keepdims=True).reshape(rows, 1)
    o_win = _dot(pw.reshape(rows, win_keys).astype(BF16), vwb_sc[pl.ds(start, win_keys), :]) * (1.0 / lw)

    gates = g_ref[...]
    for g in range(GROUP):
        r = slice(g * tq, (g + 1) * tq)
        o = (gates[:, g:g + 1] * o_cmp[r] + gates[:, GROUP + g:GROUP + g + 1] * o_slc[r]
             + gates[:, 2 * GROUP + g:2 * GROUP + g + 1] * o_win[r])
        o_ref[:, g * HEAD_DIM:(g + 1) * HEAD_DIM] = o.astype(o_ref.dtype)


def _attn_prompt(z, gates, kc, vc):
    bsz, t_len, _ = z.shape
    n_slc = -(-t_len // SLC_LEN)
    assert t_len % PAGE == 0 and n_slc <= LANES and kc.shape[1] <= LANES
    n_cmp_rows = kc.shape[1]
    assert n_cmp_rows == LANES, "compressed keys are laid out on one 128-lane tile"
    kv0 = COL_KV // HEAD_DIM

    def kv_spec(kind):
        return pl.BlockSpec((None, t_len, HEAD_DIM), lambda b, k, t: (b, 0, kv0 + kind * N_KV + k))

    cmp_spec = pl.BlockSpec((None, n_cmp_rows, HEAD_DIM), lambda b, k, t: (b, 0, k))
    q_blk0 = COL_Q // (GROUP * HEAD_DIM)
    kb = min(SLC_KEY_BLOCK, t_len)
    win_keys = min(WINDOW + PAGE, t_len)
    assert t_len % kb == 0
    rows = GROUP * PAGE
    kv_scratch = pltpu.VMEM((t_len, HEAD_DIM), BF16)
    return pl.pallas_call(
        functools.partial(_attn_prompt_kernel, n_slc=n_slc, kb=kb, win_keys=win_keys),
        out_shape=jax.ShapeDtypeStruct((bsz, t_len, ATTN_WIDTH), BF16),
        grid=(bsz, N_KV, t_len // PAGE),
        in_specs=[pl.BlockSpec((None, PAGE, GROUP * HEAD_DIM), lambda b, k, t: (b, t, q_blk0 + k)),
                  cmp_spec, cmp_spec,
                  kv_spec(2), kv_spec(3), kv_spec(4), kv_spec(5),
                  pl.BlockSpec((None, PAGE, LANES), lambda b, k, t: (b, t, k))],
        out_specs=pl.BlockSpec((None, PAGE, GROUP * HEAD_DIM), lambda b, k, t: (b, t, k)),
        scratch_shapes=[kv_scratch, kv_scratch, kv_scratch, kv_scratch,
                        pltpu.VMEM((rows, t_len), F32),
                        pltpu.VMEM((rows, LANES), F32),
                        pltpu.VMEM((rows, LANES), F32),
                        pltpu.VMEM((rows, HEAD_DIM), F32)],
        compiler_params=_cparams(("parallel", "parallel", "arbitrary")),
        name="attn_prompt",
    )(z, kc, vc, z, z, z, z, gates)


def _attn_sample_kernel(*refs, n_pages, n_groups, t_new, n_slc, n_win):
    page_refs = refs[1:1 + n_pages]
    zs_ref, kc_ref, vc_ref, win_ref, g_ref, o_ref, ocmp_sc, sel_sc, m_sc, l_sc, acc_sc = refs[1 + n_pages:]
    pages = [p.reshape(PAGE * 2 * N_KV, HEAD_DIM) for p in page_refs]
    win_rows = win_ref.reshape(n_win * 2 * N_KV, HEAD_DIM)
    grp = pl.program_id(1)
    tp = SUBLANES
    rows = GROUP * tp
    n_cmp_rows = kc_ref.shape[0]
    sel_lanes = sel_sc.shape[2]
    past_len = n_groups * n_pages * PAGE

    def q_heads(k):
        c0 = COL_Q + k * GROUP * HEAD_DIM
        q = zs_ref[:, c0:c0 + GROUP * HEAD_DIM] * SCALE
        return jnp.concatenate([q[:, g * HEAD_DIM:(g + 1) * HEAD_DIM] for g in range(GROUP)],
                               axis=0).astype(BF16)

    @pl.when(grp == 0)
    def _():
        ovb = _overlap_matrix((n_cmp_rows, sel_lanes), n_slc)
        mlane = lax.broadcasted_iota(I32, (rows, n_cmp_rows), 1)
        vbias = jnp.where(mlane >= 1, 0.0, NEG)
        lane = lax.broadcasted_iota(I32, (tp, sel_lanes), 1)
        lane_f = lane.astype(F32)
        tok = lax.broadcasted_iota(I32, (tp, sel_lanes), 0)
        cur = (past_len + tok) >> SLC_SHIFT
        forced = (lane == 0) | (lane == cur) | (lane == cur - 1)
        for k in range(N_KV):
            qk = q_heads(k)
            kc = kc_ref[:, k * HEAD_DIM:(k + 1) * HEAD_DIM]
            vc = vc_ref[:, k * HEAD_DIM:(k + 1) * HEAD_DIM]
            s = _dot_nt(qk, kc) + vbias
            e = jnp.exp(s - jnp.max(s, axis=1, keepdims=True))
            p = e * (1.0 / jnp.sum(e, axis=1, keepdims=True))
            ocmp_sc[k] = _dot(p.astype(BF16), vc)
            psum = p[0:tp]
            for g in range(1, GROUP):
                psum = psum + p[g * tp:(g + 1) * tp]
            imp = _split3_dot(psum, ovb)
            score = jnp.where(lane <= cur, jnp.where(forced, FORCE, imp), -1.0)
            score = jnp.where(lane < n_slc, score, -3e38)
            sel = jnp.zeros((tp, sel_lanes), F32)
            for _ in range(min(N_SEL, n_slc)):
                top = jnp.max(score, axis=1, keepdims=True)
                idx = jnp.min(jnp.where(score == top, lane_f, float(sel_lanes)), axis=1, keepdims=True)
                hit = lane_f == idx
                sel = jnp.where(hit, 1.0, sel)
                score = jnp.where(hit, -3e38, score)
            sel_sc[k] = sel
            _flash_init(m_sc, l_sc, acc_sc, k)

    nk = n_pages * PAGE
    jrow = lax.broadcasted_iota(I32, (sel_lanes, nk), 0)
    klane = lax.broadcasted_iota(I32, (sel_lanes, nk), 1)
    expand = jnp.where(jrow == grp * (nk // SLC_LEN) + (klane >> SLC_SHIFT), 1.0, 0.0).astype(BF16)
    for k in range(N_KV):
        kk = jnp.concatenate([p[pl.ds(k, PAGE, stride=2 * N_KV), :].astype(BF16) for p in pages], axis=0)
        vv = jnp.concatenate([p[pl.ds(N_KV + k, PAGE, stride=2 * N_KV), :].astype(BF16) for p in pages], axis=0)
        selm = _dot(sel_sc[k].astype(BF16), expand)
        bias = jnp.where(selm > 0.5, 0.0, NEG)
        bias = jnp.concatenate([bias] * GROUP, axis=0)
        _flash_update(m_sc, l_sc, acc_sc, k, 0, rows, _dot_nt(q_heads(k), kk) + bias, vv)

    @pl.when(grp == n_groups - 1)
    def _():
        tr = lax.broadcasted_iota(I32, (rows, tp), 0) % tp
        tc = lax.broadcasted_iota(I32, (rows, tp), 1)
        new_bias = jnp.where((tc <= tr) & (tc < t_new), 0.0, NEG)
        wr = lax.broadcasted_iota(I32, (rows, n_win), 0) % tp
        wc = lax.broadcasted_iota(I32, (rows, n_win), 1)
        wdiff = wr + n_win - wc
        win_bias = jnp.where((wdiff >= 0) & (wdiff < WINDOW), 0.0, NEG)
        gates = g_ref[...]
        for k in range(N_KV):
            qk = q_heads(k)

            def new_rows(kind, k=k):
                c0 = COL_KV + kind * KV_WIDTH + k * HEAD_DIM
                return zs_ref[:, c0:c0 + HEAD_DIM].astype(BF16)

            _flash_update(m_sc, l_sc, acc_sc, k, 0, rows, _dot_nt(qk, new_rows(2)) + new_bias, new_rows(3))
            o_slc = acc_sc[k] * (1.0 / l_sc[k])
            kw = win_rows[pl.ds(k, n_win, stride=2 * N_KV), :].astype(BF16)
            vw = win_rows[pl.ds(N_KV + k, n_win, stride=2 * N_KV), :].astype(BF16)
            s1 = _dot_nt(qk, kw) + win_bias
            s2 = _dot_nt(qk, new_rows(4)) + new_bias
            mx = jnp.maximum(jnp.max(s1, axis=1, keepdims=True), jnp.max(s2, axis=1, keepdims=True))
            p1 = jnp.exp(s1 - mx)
            p2 = jnp.exp(s2 - mx)
            den = jnp.sum(p1, axis=1, keepdims=True) + jnp.sum(p2, axis=1, keepdims=True)
            o_win = (_dot(p1.astype(BF16), vw) + _dot(p2.astype(BF16), new_rows(5))) * (1.0 / den)
            o_cmp = ocmp_sc[k]
            for g in range(GROUP):
                r = slice(g * tp, (g + 1) * tp)
                c = k * LANES
                o = (gates[:, c + g:c + g + 1] * o_cmp[r]
                     + gates[:, c + GROUP + g:c + GROUP + g + 1] * o_slc[r]
                     + gates[:, c + 2 * GROUP + g:c + 2 * GROUP + g + 1] * o_win[r])
                h = k * GROUP + g
                o_ref[:, h * HEAD_DIM:(h + 1) * HEAD_DIM] = o.astype(o_ref.dtype)


def _attn_sample(c5, layer, page_table, zs, gates, kc, vc, win5, t_new, n_pages):
    bsz, tot_pages = page_table.shape
    n_groups = tot_pages // n_pages
    past_len = tot_pages * PAGE
    n_slc = -(-(past_len + t_new) // SLC_LEN)
    sel_lanes = -(-n_slc // LANES) * LANES
    n_win = win5.shape[2]
    tp = SUBLANES
    rows = GROUP * tp
    page_specs = [pl.BlockSpec((None, None, PAGE, 2 * N_KV, HEAD_DIM),
                               functools.partial(lambda b, g, pt, i: (layer, pt[b, g * n_pages + i], 0, 1, 0), i=i))
                  for i in range(n_pages)]

    def bmap(b, g, pt):
        return (b, 0, 0)

    grid_spec = pltpu.PrefetchScalarGridSpec(
        num_scalar_prefetch=1,
        grid=(bsz, n_groups),
        in_specs=page_specs + [
            pl.BlockSpec((None, tp, zs.shape[2]), bmap),
            pl.BlockSpec((None, kc.shape[1], KV_WIDTH), bmap),
            pl.BlockSpec((None, vc.shape[1], KV_WIDTH), bmap),
            pl.BlockSpec((None, None, n_win, 2 * N_KV, HEAD_DIM), lambda b, g, pt: (layer, b, 0, 0, 0)),
            pl.BlockSpec((None, tp, N_KV * LANES), bmap)],
        out_specs=pl.BlockSpec((None, tp, ATTN_WIDTH), bmap),
        scratch_shapes=[pltpu.VMEM((N_KV, rows, HEAD_DIM), F32),
                        pltpu.VMEM((N_KV, tp, sel_lanes), F32),
                        pltpu.VMEM((N_KV, rows, LANES), F32),
                        pltpu.VMEM((N_KV, rows, LANES), F32),
                        pltpu.VMEM((N_KV, rows, HEAD_DIM), F32)],
    )
    return pl.pallas_call(
        functools.partial(_attn_sample_kernel, n_pages=n_pages, n_groups=n_groups, t_new=t_new,
                          n_slc=n_slc, n_win=n_win),
        out_shape=jax.ShapeDtypeStruct((bsz, tp, ATTN_WIDTH), BF16),
        grid_spec=grid_spec,
        compiler_params=_cparams(("parallel", "arbitrary")),
        name="attn_sample",
    )(page_table, *([c5] * n_pages), zs, kc, vc, win5, gates)


CONV_HALO = 32
CONV_LANE_CHUNK = 256


def _conv_kernel(*refs, tc, multi_tile):
    if multi_tile:
        u_ref, up_ref, buf_ref, dw_ref, db_ref, lg_ref, lb_ref, o_ref, tail_ref, ext_sc, y_sc = refs
    else:
        u_ref, buf_ref, dw_ref, db_ref, lg_ref, lb_ref, o_ref, tail_ref, ext_sc, y_sc = refs
    t = pl.program_id(1)

    def glu(u):
        return u[:, :CONV_CH] * jax.nn.sigmoid(u[:, CONV_CH:])

    @pl.when(t == 0)
    def _():
        ext_sc[0:CONV_HALO, :] = buf_ref[...]

    if multi_tile:
        @pl.when(t > 0)
        def _():
            ext_sc[0:CONV_HALO, :] = glu(up_ref[...])

    ext_sc[CONV_HALO:CONV_HALO + tc, :] = glu(u_ref[...])
    off = CONV_HALO - (CONV_WIDTH - 1)
    for c0 in range(0, CONV_CH, CONV_LANE_CHUNK):
        cs = slice(c0, c0 + CONV_LANE_CHUNK)
        acc = jnp.broadcast_to(db_ref[:, cs], (tc, CONV_LANE_CHUNK))
        for w in range(CONV_WIDTH):
            acc = acc + ext_sc[off + w:off + w + tc, cs] * dw_ref[w:w + 1, cs]
        y_sc[:, cs] = acc
    y = y_sc[...]
    mu = jnp.mean(y, axis=-1, keepdims=True)
    d = y - mu
    var = jnp.mean(d * d, axis=-1, keepdims=True)
    yn = d * lax.rsqrt(var + LN_EPS) * lg_ref[...] + lb_ref[...]
    o_ref[...] = (yn * jax.nn.sigmoid(yn)).astype(o_ref.dtype)

    @pl.when(t == pl.num_programs(1) - 1)
    def _():
        tail_ref[...] = ext_sc[tc:tc + CONV_HALO, :]


def _conv_group(z, buf, dw_w, dw_b, ln_g, ln_b, tc):
    bsz, t_len, _ = z.shape
    n_t = t_len // tc
    multi_tile = n_t > 1
    assert COL_U == 0 and (not multi_tile or tc % CONV_HALO == 0)
    halo_per_tile = max(tc // CONV_HALO, 1)

    def vec(v):
        return v.reshape(1, CONV_CH)

    cmap = lambda b, t: (0, 0)
    in_specs = [pl.BlockSpec((None, tc, 2 * CONV_CH), lambda b, t: (b, t, 0))]
    args = [z]
    if multi_tile:
        in_specs.append(pl.BlockSpec((None, CONV_HALO, 2 * CONV_CH),
                                     lambda b, t: (b, jnp.maximum(t * halo_per_tile - 1, 0), 0)))
        args.append(z)
    in_specs += [pl.BlockSpec((None, CONV_HALO, CONV_CH), lambda b, t: (b, 0, 0)),
                 pl.BlockSpec((CONV_WIDTH, CONV_CH), cmap),
                 pl.BlockSpec((1, CONV_CH), cmap), pl.BlockSpec((1, CONV_CH), cmap),
                 pl.BlockSpec((1, CONV_CH), cmap)]
    args += [buf, dw_w, vec(dw_b), vec(ln_g), vec(ln_b)]
    return pl.pallas_call(
        functools.partial(_conv_kernel, tc=tc, multi_tile=multi_tile),
        out_shape=(jax.ShapeDtypeStruct((bsz, t_len, CONV_CH), BF16),
                   jax.ShapeDtypeStruct((bsz, CONV_HALO, CONV_CH), F32)),
        grid=(bsz, n_t),
        in_specs=in_specs,
        out_specs=(pl.BlockSpec((None, tc, CONV_CH), lambda b, t: (b, t, 0)),
                   pl.BlockSpec((None, CONV_HALO, CONV_CH), lambda b, t: (b, 0, 0))),
        scratch_shapes=[pltpu.VMEM((CONV_HALO + tc, CONV_CH), F32), pltpu.VMEM((tc, CONV_CH), F32)],
        compiler_params=_cparams(("parallel", "arbitrary")),
        name="conv_group",
    )(*args)


def _rope_tables(pos):
    half = ROT_DIM // 2
    inv = ROPE_THETA ** (-jnp.arange(half, dtype=F32) / half)
    ang = pos.astype(F32)[:, None] * inv[None, :]
    cos, sin = jnp.cos(ang), jnp.sin(ang)
    n = pos.shape[0]
    zeros = jnp.zeros((n, half), F32)
    rest = HEAD_DIM - ROT_DIM
    c = jnp.concatenate([cos, cos, jnp.ones((n, rest), F32)], axis=1)
    s1 = jnp.concatenate([zeros, sin, jnp.zeros((n, rest), F32)], axis=1)
    s2 = jnp.concatenate([-sin, zeros, jnp.zeros((n, rest), F32)], axis=1)
    return c, s1, s2


def _prep_layer(w_in_l, w_out_l, w_gate_l, w_up_l, w_down_l, cmp_pe_l, cmp_w1_l, cmp_w2_l, cmp_b2_l):
    d = w_in_l.shape[0]
    kv_end = ATTN_WIDTH + KV_COLS
    gate_end = kv_end + 3 * N_HEADS
    w_main = jnp.concatenate([w_in_l[:, gate_end:], w_in_l[:, :kv_end]], axis=1).astype(BF16)
    wg = w_in_l[:, kv_end:gate_end].reshape(d, 3, N_KV, GROUP).transpose(0, 2, 1, 3).reshape(d, N_KV, 3 * GROUP)
    wg = jnp.pad(wg, ((0, 0), (0, 0), (0, LANES - 3 * GROUP))).reshape(d, N_KV * LANES).astype(BF16)
    w_gate_b = w_gate_l.astype(BF16)
    w_up_b = w_up_l.astype(BF16)
    w_down_b = w_down_l.astype(BF16)
    pef = jnp.pad(cmp_pe_l.reshape(2, 1, CMP_LEN * HEAD_DIM), ((0, 0), (0, SUBLANES - 1), (0, 0))).astype(BF16)
    w1 = cmp_w1_l.astype(BF16).reshape(2, CMP_LEN // CMP_STRIDE, CMP_STRIDE * HEAD_DIM, CMP_HIDDEN)
    cw = (pef, w1, cmp_w2_l.astype(BF16), cmp_b2_l.reshape(2, 1, HEAD_DIM))
    return w_main, wg, w_out_l.astype(BF16), w_gate_b, w_up_b, w_down_b, cw


def _largest_tile(n, unit, cap):
    best = None
    for t in range(unit, min(n, cap) + 1, unit):
        if n % t == 0:
            best = t
    assert best is not None, (n, unit, cap)
    return best


def _dense_tail(x, attn, conv, w_out_b, norm_ffn_l, w_gate_b, w_up_b, w_down_b, tm):
    d = x.shape[1]
    d_ff = w_gate_b.shape[1]
    h = _outproj(attn, conv, w_out_b, x, tm, 512)
    hn = _rmsnorm(h, norm_ffn_l, BF16)
    act = _ffn_up(hn, w_gate_b, w_up_b, tm, _largest_tile(d_ff, 2 * LANES, 512))
    return _ffn_down(act, w_down_b, h, min(tm, 512), min(d, 1024), _largest_tile(d_ff, LANES, 6144))


def kernel(x_prompt, x_sample, cache_kv, cache_win, state_conv, page_table, norm_mix, w_in, cmp_pe, cmp_w1,
           cmp_w2, cmp_b2, conv_dw_w, conv_dw_b, conv_ln_g, conv_ln_b, w_out, norm_ffn, w_gate, w_up, w_down,
           norm_final):
    depth = w_in.shape[0]
    bsz, seq, d = x_prompt.shape
    dec_b, dec_t, _ = x_sample.shape
    n_pool, page_size = cache_kv.shape[1], cache_kv.shape[2]
    tot_pages = page_table.shape[1]
    past_len = tot_pages * page_size
    assert page_size == PAGE and seq % PAGE == 0 and dec_t <= SUBLANES
    assert conv_dw_w.shape[2] == CONV_CH and d == ATTN_WIDTH + CONV_CH
    pages_per_group = min(16, tot_pages, seq // PAGE)
    assert tot_pages % pages_per_group == 0 and (seq // PAGE) % pages_per_group == 0
    tp = SUBLANES
    ms = dec_b * tp

    tm_p = min(512, seq)
    tabs_p = _rope_tables(jnp.arange(seq, dtype=I32))
    tabs_s = tuple(jnp.tile(t, (dec_b, 1)) for t in _rope_tables(past_len + jnp.arange(tp, dtype=I32)))

    hp = x_prompt.reshape(bsz * seq, d)
    hs = jnp.pad(x_sample, ((0, 0), (0, tp - dec_t), (0, 0))).reshape(ms, d)
    cache5 = cache_kv.reshape(depth, n_pool, page_size, 4 * N_KV, HEAD_DIM)
    win5 = cache_win.reshape(depth, dec_b, cache_win.shape[2], 2 * N_KV, HEAD_DIM)
    zero_buf = jnp.zeros((bsz, CONV_HALO, CONV_CH), F32)
    tm_dense = min(1024, bsz * seq)

    kvp, winp, convp, kvs, wins, convs = [], [], [], [], [], []
    for l in range(depth):
        w_main, wg, w_out_b, w_gate_b, w_up_b, w_down_b, cw = _prep_layer(
            w_in[l], w_out[l], w_gate[l], w_up[l], w_down[l], cmp_pe[l], cmp_w1[l], cmp_w2[l], cmp_b2[l])

        xn = _rmsnorm(hp, norm_mix[l], BF16)
        z, gates = _inproj(xn, w_main, wg, tabs_p, seq // tm_p, tm_p)
        z3 = z.reshape(bsz, seq, z.shape[1])
        kc, vc = _compress_prompt(z3, cw, pages_per_group)
        attn = _attn_prompt(z3, gates.reshape(bsz, seq, N_KV * LANES), kc, vc)
        conv, tail = _conv_group(z3, zero_buf, conv_dw_w[l], conv_dw_b[l], conv_ln_g[l], conv_ln_b[l], PAGE)
        hp = _dense_tail(hp, attn.reshape(bsz * seq, ATTN_WIDTH), conv.reshape(bsz * seq, CONV_CH), w_out_b,
                         norm_ffn[l], w_gate_b, w_up_b, w_down_b, tm_dense)
        kvp.append(z3[:, :, COL_KV:COL_KV + 4 * KV_WIDTH].reshape(bsz, seq, 4, N_KV, HEAD_DIM))
        keep = min(WINDOW, seq)
        winp.append(z3[:, seq - keep:, COL_KV + 4 * KV_WIDTH:COL_KV + KV_COLS].reshape(bsz, keep, 2, N_KV, HEAD_DIM))
        convp.append(tail[:, CONV_HALO - (CONV_WIDTH - 1):])

        xn_s = _rmsnorm(hs, norm_mix[l], BF16)
        zs, gates_s = _inproj(xn_s, w_main, wg, tabs_s, 1, ms)
        zs3 = zs.reshape(dec_b, tp, zs.shape[1])
        kc_s, vc_s = _compress_sample(cache5, l, page_table, cw, pages_per_group)
        attn_s = _attn_sample(cache5, l, page_table, zs3, gates_s.reshape(dec_b, tp, N_KV * LANES),
                              kc_s, vc_s, win5, dec_t, pages_per_group)
        buf_s = jnp.pad(state_conv[l], ((0, 0), (CONV_HALO - (CONV_WIDTH - 1), 0), (0, 0)))
        conv_s, tail_s = _conv_group(zs3, buf_s, conv_dw_w[l], conv_dw_b[l], conv_ln_g[l], conv_ln_b[l], tp)
        hs = _dense_tail(hs, attn_s.reshape(ms, ATTN_WIDTH), conv_s.reshape(ms, CONV_CH), w_out_b, norm_ffn[l],
                         w_gate_b, w_up_b, w_down_b, ms)
        kvs.append(zs3[:, :dec_t, COL_KV:COL_KV + 4 * KV_WIDTH].reshape(dec_b, dec_t, 4, N_KV, HEAD_DIM))
        new_win_s = zs3[:, :dec_t, COL_KV + 4 * KV_WIDTH:COL_KV + KV_COLS].reshape(dec_b, dec_t, 2, N_KV, HEAD_DIM)
        win_all = jnp.concatenate([cache_win[l], new_win_s], axis=1)
        wins.append(win_all[:, win_all.shape[1] - min(WINDOW, win_all.shape[1]):])
        new_glu = tail_s[:, CONV_HALO - tp:CONV_HALO - tp + dec_t]
        ext_s = jnp.concatenate([state_conv[l], new_glu], axis=1)
        convs.append(ext_s[:, ext_s.shape[1] - (CONV_WIDTH - 1):])

    y_prompt = _rmsnorm(hp, norm_final, F32).reshape(bsz, seq, d)
    y_sample = _rmsnorm(hs, norm_final, F32).reshape(dec_b, tp, d)[:, :dec_t]
    return (y_prompt, y_sample, jnp.stack(kvp), jnp.stack(winp), jnp.stack(convp),
            jnp.stack(kvs), jnp.stack(wins), jnp.stack(convs))
```

```python
import functools
import math

import jax
import jax.numpy as jnp
from jax import lax
from jax.experimental import pallas as pl
from jax.experimental.pallas import tpu as pltpu

F32 = jnp.float32
BF16 = jnp.bfloat16
I32 = jnp.int32

N_HEADS = 16
N_KV = 4
HEAD_DIM = 128
GROUP = N_HEADS // N_KV
ATTN_WIDTH = N_HEADS * HEAD_DIM
KV_WIDTH = N_KV * HEAD_DIM
CONV_CH = 2048
CONV_WIDTH = 31
ROT_DIM = HEAD_DIM // 4
ROPE_THETA = 500000.0
CMP_LEN = 32
CMP_STRIDE = 16
CMP_HIDDEN = 2 * HEAD_DIM
SLC_LEN = 64
SLC_SHIFT = 6
N_SEL = 16
WINDOW = 512
SCALE = HEAD_DIM ** -0.5
LOG2E = 1.0 / math.log(2.0)
SLC_KEY_BLOCK = 512
NEG = -1e30
FORCE = 1e9
RMS_EPS = 1e-6
LN_EPS = 1e-5

LANES = 128
SUBLANES = 8
VMEM_LIMIT = 56 * 1024 * 1024

PAGE = 128
CHUNKS_PER_PAGE = PAGE // CMP_STRIDE
COL_U = 0
COL_Q = 2 * CONV_CH
COL_KV = COL_Q + ATTN_WIDTH
KV_COLS = 6 * KV_WIDTH
PROJ_TN = 512


def _cparams(sem):
    return pltpu.CompilerParams(dimension_semantics=sem, vmem_limit_bytes=VMEM_LIMIT)


def _dot(a, b):
    return jnp.dot(a, b, preferred_element_type=F32)


def _dot_nt(a, b):
    return lax.dot_general(a, b, (((1,), (1,)), ((), ())), preferred_element_type=F32)


def _split3_dot(x, m_bf16):
    hi = x.astype(BF16)
    r1 = x - hi.astype(F32)
    mid = r1.astype(BF16)
    lo = (r1 - mid.astype(F32)).astype(BF16)
    return _dot(hi, m_bf16) + _dot(mid, m_bf16) + _dot(lo, m_bf16)


def _rmsnorm_kernel(x_ref, g_ref, o_ref):
    x = x_ref[...]
    ms = jnp.mean(x * x, axis=-1, keepdims=True)
    o_ref[...] = (x * lax.rsqrt(ms + RMS_EPS) * g_ref[...]).astype(o_ref.dtype)


def _rmsnorm(x, g, out_dtype):
    m, d = x.shape
    tm = min(m, 256)
    return pl.pallas_call(
        _rmsnorm_kernel,
        out_shape=jax.ShapeDtypeStruct((m, d), out_dtype),
        grid=(m // tm,),
        in_specs=[pl.BlockSpec((tm, d), lambda i: (i, 0)),
                  pl.BlockSpec((1, d), lambda i: (0, 0))],
        out_specs=pl.BlockSpec((tm, d), lambda i: (i, 0)),
        compiler_params=_cparams(("parallel",)),
        name="rmsnorm",
    )(x, g.reshape(1, d))


def _inproj_kernel(x_ref, gn_ref, w_ref, wg_ref, c_ref, s1_ref, s2_ref, z_ref, g_ref, xn_sc):
    j = pl.program_id(1)

    @pl.when(j == 0)
    def _():
        x = x_ref[...]
        ms = jnp.mean(x * x, axis=-1, keepdims=True)
        xn_sc[...] = (x * lax.rsqrt(ms + RMS_EPS) * gn_ref[...]).astype(xn_sc.dtype)

    a = xn_sc[...]
    acc = _dot(a, w_ref[...])
    q0 = COL_Q // PROJ_TN
    k0 = COL_KV // PROJ_TN
    is_rope = ((j >= q0) & (j <= k0)) | (j == k0 + 2) | (j == k0 + 4)

    @pl.when(is_rope)
    def _():
        c = c_ref[...]
        s1 = s1_ref[...]
        s2 = s2_ref[...]
        for h in range(PROJ_TN // HEAD_DIM):
            x = acc[:, h * HEAD_DIM:(h + 1) * HEAD_DIM]
            z_ref[:, h * HEAD_DIM:(h + 1) * HEAD_DIM] = (
                x * c + pltpu.roll(x, ROT_DIM // 2, 1) * s1
                + pltpu.roll(x, HEAD_DIM - ROT_DIM // 2, 1) * s2)

    @pl.when(jnp.logical_not(is_rope))
    def _():
        z_ref[...] = acc

    @pl.when(j == 0)
    def _():
        g_ref[...] = jax.nn.sigmoid(_dot(a, wg_ref[...]))


def _inproj(x, g_norm, w, wg, tabs, tab_tiles, tm):
    m, k = x.shape
    n = w.shape[1]
    c, s1, s2 = tabs
    tab_spec = pl.BlockSpec((tm, HEAD_DIM), lambda i, j: (i % tab_tiles, 0))
    return pl.pallas_call(
        _inproj_kernel,
        out_shape=(jax.ShapeDtypeStruct((m, n), F32),
                   jax.ShapeDtypeStruct((m, N_KV * LANES), F32)),
        grid=(m // tm, n // PROJ_TN),
        in_specs=[pl.BlockSpec((tm, k), lambda i, j: (i, 0)),
                  pl.BlockSpec((1, k), lambda i, j: (0, 0)),
                  pl.BlockSpec((k, PROJ_TN), lambda i, j: (0, j)),
                  pl.BlockSpec((k, N_KV * LANES), lambda i, j: (0, 0)),
                  tab_spec, tab_spec, tab_spec],
        out_specs=(pl.BlockSpec((tm, PROJ_TN), lambda i, j: (i, j)),
                   pl.BlockSpec((tm, N_KV * LANES), lambda i, j: (i, 0))),
        scratch_shapes=[pltpu.VMEM((tm, k), BF16)],
        compiler_params=_cparams(("parallel", "arbitrary")),
        name="inproj",
    )(x, g_norm.reshape(1, k), w, wg, c, s1, s2)


def _rows_out_kernel(*refs, depth):
    z_refs = refs[:depth]
    o_ref = refs[depth]
    layer = pl.program_id(0)
    for li, z_ref in enumerate(z_refs):
        @pl.when(layer == li)
        def _(z_ref=z_ref):
            for r in range(o_ref.shape[1]):
                o_ref[:, r, :] = z_ref[:, r * HEAD_DIM:(r + 1) * HEAD_DIM]


def _rows_out(zs, n_tiles, tm, col0, n_head_rows, row_tile_of):
    depth = len(zs)
    width = n_head_rows * HEAD_DIM
    assert col0 % width == 0
    in_specs = [pl.BlockSpec((tm, width),
                             functools.partial(lambda l, i, li: (jnp.where(l == li, row_tile_of(i), 0), col0 // width),
                                               li=li))
                for li in range(depth)]
    return pl.pallas_call(
        functools.partial(_rows_out_kernel, depth=depth),
        out_shape=jax.ShapeDtypeStruct((depth, n_tiles * tm, n_head_rows, HEAD_DIM), F32),
        grid=(depth, n_tiles),
        in_specs=in_specs,
        out_specs=pl.BlockSpec((None, tm, n_head_rows, HEAD_DIM), lambda l, i: (l, i, 0, 0)),
        compiler_params=_cparams(("arbitrary", "arbitrary")),
        name="rows_out",
    )(*zs)


def _outproj_kernel(a1_ref, a2_ref, w1_ref, w2_ref, r_ref, o_ref):
    o_ref[...] = r_ref[...] + _dot(a1_ref[...], w1_ref[...]) + _dot(a2_ref[...], w2_ref[...])


def _outproj(a1, a2, w, res, tm, tn):
    m, k1 = a1.shape
    k2 = a2.shape[1]
    n = w.shape[1]
    assert k1 == k2 and w.shape[0] == k1 + k2
    return pl.pallas_call(
        _outproj_kernel,
        out_shape=jax.ShapeDtypeStruct((m, n), F32),
        grid=(m // tm, n // tn),
        in_specs=[pl.BlockSpec((tm, k1), lambda i, j: (i, 0)),
                  pl.BlockSpec((tm, k2), lambda i, j: (i, 0)),
                  pl.BlockSpec((k1, tn), lambda i, j: (0, j)),
                  pl.BlockSpec((k2, tn), lambda i, j: (1, j)),
                  pl.BlockSpec((tm, tn), lambda i, j: (i, j))],
        out_specs=pl.BlockSpec((tm, tn), lambda i, j: (i, j)),
        compiler_params=_cparams(("parallel", "arbitrary")),
        name="outproj",
    )(a1, a2, w, w, res)


def _ffn_up_kernel(a_ref, wg_ref, wu_ref, o_ref):
    a = a_ref[...]
    gate = _dot(a, wg_ref[...])
    up = _dot(a, wu_ref[...])
    o_ref[...] = (gate * jax.nn.sigmoid(gate) * up).astype(o_ref.dtype)


def _ffn_up(a, wg, wu, tm, tn):
    m, k = a.shape
    n = wg.shape[1]
    return pl.pallas_call(
        _ffn_up_kernel,
        out_shape=jax.ShapeDtypeStruct((m, n), BF16),
        grid=(m // tm, n // tn),
        in_specs=[pl.BlockSpec((tm, k), lambda i, j: (i, 0)),
                  pl.BlockSpec((k, tn), lambda i, j: (0, j)),
                  pl.BlockSpec((k, tn), lambda i, j: (0, j))],
        out_specs=pl.BlockSpec((tm, tn), lambda i, j: (i, j)),
        compiler_params=_cparams(("parallel", "arbitrary")),
        name="ffn_up",
    )(a, wg, wu)


def _ffn_down_kernel(a_ref, w_ref, r_ref, o_ref, acc_ref):
    kk = pl.program_id(2)

    @pl.when(kk == 0)
    def _():
        acc_ref[...] = r_ref[...]

    acc_ref[...] += _dot(a_ref[...], w_ref[...])

    @pl.when(kk == pl.num_programs(2) - 1)
    def _():
        o_ref[...] = acc_ref[...]


def _ffn_down(a, w, res, tm, tn, tk):
    m, k = a.shape
    n = w.shape[1]
    return pl.pallas_call(
        _ffn_down_kernel,
        out_shape=jax.ShapeDtypeStruct((m, n), F32),
        grid=(m // tm, n // tn, k // tk),
        in_specs=[pl.BlockSpec((tm, tk), lambda i, j, kk: (i, kk)),
                  pl.BlockSpec((tk, tn), lambda i, j, kk: (kk, j)),
                  pl.BlockSpec((tm, tn), lambda i, j, kk: (i, j))],
        out_specs=pl.BlockSpec((tm, tn), lambda i, j, kk: (i, j)),
        scratch_shapes=[pltpu.VMEM((tm, tn), F32)],
        compiler_params=_cparams(("parallel", "parallel", "arbitrary")),
        name="ffn_down",
    )(a, w, res)


def _gelu_tanh(x):
    return 0.5 * x * (1.0 + jnp.tanh(math.sqrt(2.0 / math.pi) * (x + 0.044715 * (x * x * x))))


def _compress_kernel(*refs, n_pages, n_prefetch):
    page_refs = refs[n_prefetch:n_prefetch + n_pages]
    pef_ref, w1_ref, w2_ref, b2_ref, kc_ref, vc_ref, carry_ref = refs[n_prefetch + n_pages:]
    grp = pl.program_id(1)
    n_chunks = n_pages * CHUNKS_PER_PAGE
    half_k = CMP_STRIDE * HEAD_DIM

    @pl.when(grp == 0)
    def _():
        carry_ref[...] = jnp.zeros_like(carry_ref)

    rows = N_KV * n_chunks
    pairs = CHUNKS_PER_PAGE // 2
    lo = lax.broadcasted_iota(I32, (pairs, CMP_STRIDE, 2 * N_KV, HEAD_DIM), 2) < N_KV
    xk, xv = [], []
    for p in page_refs:
        x = p[...].reshape(pairs, 2, CMP_STRIDE, 2 * N_KV, HEAD_DIM)
        even, odd = x[:, 0], x[:, 1]
        kp = jnp.where(lo, even, pltpu.roll(odd, N_KV, 2))
        vp = jnp.where(lo, pltpu.roll(even, N_KV, 2), odd)
        for t, dst in ((kp, xk), (vp, xv)):
            dst.append(jnp.concatenate([t[:, j] for j in range(CMP_STRIDE)], axis=-1)
                       .reshape(pairs * 2 * N_KV, CMP_STRIDE * HEAD_DIM))
    row = lax.broadcasted_iota(I32, (rows, CMP_HIDDEN), 0)
    for kind, parts, out_ref in ((0, xk, kc_ref), (1, xv, vc_ref)):
        x_all = jnp.concatenate(parts, axis=0).astype(BF16)
        a0 = _dot(x_all, w1_ref[kind, 0])
        a1 = _dot(x_all, w1_ref[kind, 1])
        pe = pef_ref[kind]
        pe_term = (_dot(pe[:, :half_k], w1_ref[kind, 0]) + _dot(pe[:, half_k:], w1_ref[kind, 1]))[0:1, :]
        shifted = pltpu.roll(jnp.where(row >= rows - N_KV, carry_ref[kind], a0), N_KV, 0)
        pre = shifted + a1 + pe_term
        out_ref[...] = (_dot(_gelu_tanh(pre).astype(BF16), w2_ref[kind]) + b2_ref[kind]).astype(out_ref.dtype)
        carry_ref[kind] = a0


def _compress_prompt(z3, cw, n_pages):
    bsz, t_len, cols = z3.shape
    z4 = z3.reshape(bsz, t_len, cols // HEAD_DIM, HEAD_DIM)
    blk = COL_KV // (2 * KV_WIDTH)
    specs = [pl.BlockSpec((None, PAGE, 2 * N_KV, HEAD_DIM),
                          functools.partial(lambda b, g, i: (b, g * n_pages + i, blk, 0), i=i))
             for i in range(n_pages)]
    return _compress(z4, specs, bsz, (t_len // PAGE) // n_pages, n_pages, cw)


def _compress_sample(c5, layer, page_table, cw, n_pages):
    bsz, tot_pages = page_table.shape
    specs = [pl.BlockSpec((None, None, PAGE, 2 * N_KV, HEAD_DIM),
                          functools.partial(lambda b, g, pt, i: (layer, pt[b, g * n_pages + i], 0, 0, 0), i=i))
             for i in range(n_pages)]
    return _compress(c5, specs, bsz, tot_pages // n_pages, n_pages, cw, prefetch=(page_table,))


def _compress(src, page_specs, n_batch, n_groups, n_pages, cw, prefetch=()):
    pef, w1, w2, b2 = cw
    n_chunks = n_pages * CHUNKS_PER_PAGE

    def const_spec(x):
        return pl.BlockSpec(x.shape, lambda b, g, *pt: (0,) * x.ndim)

    out_spec = pl.BlockSpec((None, N_KV * n_chunks, HEAD_DIM), lambda b, g, *pt: (b, g, 0))
    out_shape = jax.ShapeDtypeStruct((n_batch, n_groups * N_KV * n_chunks, HEAD_DIM), BF16)
    grid_spec = pltpu.PrefetchScalarGridSpec(
        num_scalar_prefetch=len(prefetch),
        grid=(n_batch, n_groups),
        in_specs=list(page_specs) + [const_spec(pef), const_spec(w1), const_spec(w2), const_spec(b2)],
        out_specs=(out_spec, out_spec),
        scratch_shapes=[pltpu.VMEM((2, N_KV * n_chunks, CMP_HIDDEN), F32)],
    )
    kc, vc = pl.pallas_call(
        functools.partial(_compress_kernel, n_pages=n_pages, n_prefetch=len(prefetch)),
        out_shape=(out_shape, out_shape),
        grid_spec=grid_spec,
        compiler_params=_cparams(("parallel", "arbitrary")),
        name="compress",
    )(*prefetch, *([src] * n_pages), pef, w1, w2, b2)
    return (kc.reshape(n_batch, n_groups * n_chunks, KV_WIDTH), vc.reshape(n_batch, n_groups * n_chunks, KV_WIDTH))


def _flash_init(m_sc, l_sc, acc_sc, br):
    m_sc[br] = jnp.full(m_sc.shape[1:], -jnp.inf, F32)
    l_sc[br] = jnp.zeros(l_sc.shape[1:], F32)
    acc_sc[br] = jnp.zeros(acc_sc.shape[1:], F32)


def _flash_update(m_sc, l_sc, acc_sc, br, r0, nr, s, v):
    m_old = m_sc[br, r0:r0 + nr, :]
    m_new = jnp.maximum(m_old, jnp.max(s, axis=1, keepdims=True))
    alpha = jnp.exp(m_old - m_new)
    p = jnp.exp(s - (m_new if s.shape[1] == LANES else m_new[:, 0:1]))
    l_sc[br, r0:r0 + nr, :] = alpha * l_sc[br, r0:r0 + nr, :] + jnp.sum(p, axis=1, keepdims=True)
    acc_sc[br, r0:r0 + nr, :] = alpha * acc_sc[br, r0:r0 + nr, :] + _dot(p.astype(BF16), v)
    m_sc[br, r0:r0 + nr, :] = m_new


def _overlap_matrix(shape, n_slc):
    m = lax.broadcasted_iota(I32, shape, 0)
    j = lax.broadcasted_iota(I32, shape, 1)
    start = (m - 1) * CMP_STRIDE
    ov = (m >= 1) & (start < j * SLC_LEN + SLC_LEN) & (start + CMP_LEN > j * SLC_LEN) & (j < n_slc)
    return jnp.where(ov, 1.0, 0.0).astype(BF16)


def _lane_tile_reduce(x, op):
    out = x[..., 0:LANES]
    for i in range(1, x.shape[-1] // LANES):
        out = op(out, x[..., i * LANES:(i + 1) * LANES])
    return out


def _attn_prompt_kernel(q_ref, kc_ref, vc_ref, ks_ref, vs_ref, kw_ref, vw_ref, g_ref, o_ref,
                        ksb_sc, vsb_sc, kwb_sc, vwb_sc, s_sc, m_sc, l_sc, acc_sc, *, n_slc, kb, win_keys):
    qt = pl.program_id(2)
    tq = PAGE
    rows = GROUP * tq
    t_len = ks_ref.shape[0]

    @pl.when(qt == 0)
    def _():
        ksb_sc[...] = ks_ref[...].astype(BF16)
        vsb_sc[...] = vs_ref[...].astype(BF16)
        kwb_sc[...] = kw_ref[...].astype(BF16)
        vwb_sc[...] = vw_ref[...].astype(BF16)

    q = q_ref[...] * (SCALE * LOG2E)
    q_all = jnp.concatenate([q[:, g * HEAD_DIM:(g + 1) * HEAD_DIM] for g in range(GROUP)],
                            axis=0).astype(BF16)
    row = lax.broadcasted_iota(I32, (tq, LANES), 0)
    lane = lax.broadcasted_iota(I32, (tq, LANES), 1)
    pos = qt * tq + row

    valid = (lane >= 1) & (lane * CMP_STRIDE + (CMP_LEN - CMP_STRIDE - 1) <= pos)
    vbias = jnp.where(valid, 0.0, NEG)
    vf = jnp.where(valid, 1.0, 0.0)
    s3 = _dot_nt(q_all, kc_ref[...]).reshape(GROUP, tq, LANES) + vbias[None]
    e = jnp.exp2(s3 - jnp.max(s3, axis=2, keepdims=True)) * vf[None]
    denom = jnp.sum(e, axis=2, keepdims=True)
    p3 = e * (1.0 / jnp.maximum(denom, 1e-30))
    o_cmp = _dot(p3.reshape(rows, LANES).astype(BF16), vc_ref[...])
    psum = p3[0] + p3[1] + p3[2] + p3[3]

    imp = _split3_dot(psum, _overlap_matrix((LANES, LANES), n_slc))
    cur = pos >> SLC_SHIFT
    forced = (lane == 0) | (lane == cur) | (lane == cur - 1)
    score = jnp.where(lane <= cur, jnp.where(forced, FORCE, imp), -1.0)
    score = jnp.where(lane < n_slc, score, -2.0)
    n_rank = -(-n_slc // SUBLANES) * SUBLANES
    score_t = score.T[0:n_rank]
    jrow = lax.broadcasted_iota(I32, (n_rank, tq), 0)
    rank_t = jnp.zeros((n_rank, tq), F32)
    for jp in range(n_slc):
        other = score_t[jp:jp + 1, :]
        tie = jnp.where(jrow > jp, 1.0, 0.0)
        rank_t = rank_t + jnp.where(other > score_t, 1.0, jnp.where(other == score_t, tie, 0.0))
    sel_t = jnp.where((rank_t < min(N_SEL, n_slc)) & (jrow < n_slc), 1.0, 0.0)
    sel = jnp.concatenate([sel_t, jnp.zeros((LANES - n_rank, tq), F32)], axis=0).T.astype(BF16)

    m_sc[...] = jnp.full(m_sc.shape, -jnp.inf, F32)
    l_sc[...] = jnp.zeros(l_sc.shape, F32)
    acc_sc[...] = jnp.zeros(acc_sc.shape, F32)
    n_kb = t_len // kb
    brow = lax.broadcasted_iota(I32, (LANES, kb), 0)
    blane = lax.broadcasted_iota(I32, (tq, kb), 1)
    bpos = qt * tq + lax.broadcasted_iota(I32, (tq, kb), 0)
    for cb in range(n_kb):
        @pl.when(cb * kb <= qt * tq)
        def _(cb=cb):
            key = cb * kb + blane
            expand = jnp.where(((cb * kb + lax.broadcasted_iota(I32, (LANES, kb), 1)) >> SLC_SHIFT) == brow,
                               1.0, 0.0).astype(BF16)
            selm = _dot(sel, expand)
            bias = jnp.where((selm > 0.5) & (key <= bpos), 0.0, NEG)
            s = _dot_nt(q_all, ksb_sc[cb * kb:(cb + 1) * kb, :]).reshape(GROUP, tq, kb) + bias[None]
            s_sc[:, cb * kb:(cb + 1) * kb] = s.reshape(rows, kb)
            m_sc[...] = jnp.maximum(m_sc[...], _lane_tile_reduce(s, jnp.maximum).reshape(rows, LANES))

    m_row = jnp.max(m_sc[...], axis=1, keepdims=True)
    for cb in range(n_kb):
        @pl.when(cb * kb <= qt * tq)
        def _(cb=cb):
            p = jnp.exp2(s_sc[:, cb * kb:(cb + 1) * kb] - m_row)
            l_sc[...] += _lane_tile_reduce(p, jnp.add)
            acc_sc[...] += _dot(p.astype(BF16), vsb_sc[cb * kb:(cb + 1) * kb, :])

    o_slc = acc_sc[...] * (1.0 / jnp.sum(l_sc[...], axis=1, keepdims=True))

    start = pl.multiple_of(jnp.maximum(qt * tq + tq - win_keys, 0), PAGE)
    wlane = lax.broadcasted_iota(I32, (tq, win_keys), 1)
    wdiff = (qt * tq + lax.broadcasted_iota(I32, (tq, win_keys), 0)) - (start + wlane)
    wbias = jnp.where((wdiff >= 0) & (wdiff < WINDOW), 0.0, NEG)
    sw = _dot_nt(q_all, kwb_sc[pl.ds(start, win_keys), :]).reshape(GROUP, tq, win_keys) + wbias[None]
    pw = jnp.exp2(sw - jnp.max(sw, axis=2, keepdims=True))
    lw = jnp.sum(pw, axis=2, keepdims=True).reshape(rows, 1)
    o_win = _dot(pw.reshape(rows, win_keys).astype(BF16), vwb_sc[pl.ds(start, win_keys), :]) * (1.0 / lw)

    gates = g_ref[...]
    for g in range(GROUP):
        r = slice(g * tq, (g + 1) * tq)
        o = (gates[:, g:g + 1] * o_cmp[r] + gates[:, GROUP + g:GROUP + g + 1] * o_slc[r]
             + gates[:, 2 * GROUP + g:2 * GROUP + g + 1] * o_win[r])
        o_ref[:, g * HEAD_DIM:(g + 1) * HEAD_DIM] = o.astype(o_ref.dtype)


def _attn_prompt(z, gates, kc, vc):
    bsz, t_len, _ = z.shape
    n_slc = -(-t_len // SLC_LEN)
    assert t_len % PAGE == 0 and n_slc <= LANES and kc.shape[1] <= LANES
    n_cmp_rows = kc.shape[1]
    assert n_cmp_rows == LANES, "compressed keys are laid out on one 128-lane tile"
    kv0 = COL_KV // HEAD_DIM

    def kv_spec(kind):
        return pl.BlockSpec((None, t_len, HEAD_DIM), lambda b, k, t: (b, 0, kv0 + kind * N_KV + k))

    cmp_spec = pl.BlockSpec((None, n_cmp_rows, HEAD_DIM), lambda b, k, t: (b, 0, k))
    q_blk0 = COL_Q // (GROUP * HEAD_DIM)
    kb = min(SLC_KEY_BLOCK, t_len)
    win_keys = min(WINDOW + PAGE, t_len)
    assert t_len % kb == 0
    rows = GROUP * PAGE
    kv_scratch = pltpu.VMEM((t_len, HEAD_DIM), BF16)
    return pl.pallas_call(
        functools.partial(_attn_prompt_kernel, n_slc=n_slc, kb=kb, win_keys=win_keys),
        out_shape=jax.ShapeDtypeStruct((bsz, t_len, ATTN_WIDTH), BF16),
        grid=(bsz, N_KV, t_len // PAGE),
        in_specs=[pl.BlockSpec((None, PAGE, GROUP * HEAD_DIM), lambda b, k, t: (b, t, q_blk0 + k)),
                  cmp_spec, cmp_spec,
                  kv_spec(2), kv_spec(3), kv_spec(4), kv_spec(5),
                  pl.BlockSpec((None, PAGE, LANES), lambda b, k, t: (b, t, k))],
        out_specs=pl.BlockSpec((None, PAGE, GROUP * HEAD_DIM), lambda b, k, t: (b, t, k)),
        scratch_shapes=[kv_scratch, kv_scratch, kv_scratch, kv_scratch,
                        pltpu.VMEM((rows, t_len), F32),
                        pltpu.VMEM((rows, LANES), F32),
                        pltpu.VMEM((rows, LANES), F32),
                        pltpu.VMEM((rows, HEAD_DIM), F32)],
        compiler_params=_cparams(("parallel", "parallel", "arbitrary")),
        name="attn_prompt",
    )(z, kc, vc, z, z, z, z, gates)


def _attn_sample_kernel(*refs, n_pages, n_groups, t_new, n_slc, n_win):
    page_refs = refs[1:1 + n_pages]
    zs_ref, kc_ref, vc_ref, win_ref, g_ref, o_ref, ocmp_sc, sel_sc, m_sc, l_sc, acc_sc = refs[1 + n_pages:]
    pages = [p.reshape(PAGE * 2 * N_KV, HEAD_DIM) for p in page_refs]
    win_rows = win_ref.reshape(n_win * 2 * N_KV, HEAD_DIM)
    grp = pl.program_id(1)
    tp = SUBLANES
    rows = GROUP * tp
    n_cmp_rows = kc_ref.shape[0]
    sel_lanes = sel_sc.shape[2]
    past_len = n_groups * n_pages * PAGE

    def q_heads(k):
        c0 = COL_Q + k * GROUP * HEAD_DIM
        q = zs_ref[:, c0:c0 + GROUP * HEAD_DIM] * SCALE
        return jnp.concatenate([q[:, g * HEAD_DIM:(g + 1) * HEAD_DIM] for g in range(GROUP)],
                               axis=0).astype(BF16)

    @pl.when(grp == 0)
    def _():
        ovb = _overlap_matrix((n_cmp_rows, sel_lanes), n_slc)
        mlane = lax.broadcasted_iota(I32, (rows, n_cmp_rows), 1)
        vbias = jnp.where(mlane >= 1, 0.0, NEG)
        lane = lax.broadcasted_iota(I32, (tp, sel_lanes), 1)
        lane_f = lane.astype(F32)
        tok = lax.broadcasted_iota(I32, (tp, sel_lanes), 0)
        cur = (past_len + tok) >> SLC_SHIFT
        forced = (lane == 0) | (lane == cur) | (lane == cur - 1)
        for k in range(N_KV):
            qk = q_heads(k)
            kc = kc_ref[:, k * HEAD_DIM:(k + 1) * HEAD_DIM]
            vc = vc_ref[:, k * HEAD_DIM:(k + 1) * HEAD_DIM]
            s = _dot_nt(qk, kc) + vbias
            e = jnp.exp(s - jnp.max(s, axis=1, keepdims=True))
            p = e * (1.0 / jnp.sum(e, axis=1, keepdims=True))
            ocmp_sc[k] = _dot(p.astype(BF16), vc)
            psum = p[0:tp]
            for g in range(1, GROUP):
                psum = psum + p[g * tp:(g + 1) * tp]
            imp = _split3_dot(psum, ovb)
            score = jnp.where(lane <= cur, jnp.where(forced, FORCE, imp), -1.0)
            score = jnp.where(lane < n_slc, score, -3e38)
            sel = jnp.zeros((tp, sel_lanes), F32)
            for _ in range(min(N_SEL, n_slc)):
                top = jnp.max(score, axis=1, keepdims=True)
                idx = jnp.min(jnp.where(score == top, lane_f, float(sel_lanes)), axis=1, keepdims=True)
                hit = lane_f == idx
                sel = jnp.where(hit, 1.0, sel)
                score = jnp.where(hit, -3e38, score)
            sel_sc[k] = sel
            _flash_init(m_sc, l_sc, acc_sc, k)

    nk = n_pages * PAGE
    jrow = lax.broadcasted_iota(I32, (sel_lanes, nk), 0)
    klane = lax.broadcasted_iota(I32, (sel_lanes, nk), 1)
    expand = jnp.where(jrow == grp * (nk // SLC_LEN) + (klane >> SLC_SHIFT), 1.0, 0.0).astype(BF16)
    for k in range(N_KV):
        kk = jnp.concatenate([p[pl.ds(k, PAGE, stride=2 * N_KV), :].astype(BF16) for p in pages], axis=0)
        vv = jnp.concatenate([p[pl.ds(N_KV + k, PAGE, stride=2 * N_KV), :].astype(BF16) for p in pages], axis=0)
        selm = _dot(sel_sc[k].astype(BF16), expand)
        bias = jnp.where(selm > 0.5, 0.0, NEG)
        bias = jnp.concatenate([bias] * GROUP, axis=0)
        _flash_update(m_sc, l_sc, acc_sc, k, 0, rows, _dot_nt(q_heads(k), kk) + bias, vv)

    @pl.when(grp == n_groups - 1)
    def _():
        tr = lax.broadcasted_iota(I32, (rows, tp), 0) % tp
        tc = lax.broadcasted_iota(I32, (rows, tp), 1)
        new_bias = jnp.where((tc <= tr) & (tc < t_new), 0.0, NEG)
        wr = lax.broadcasted_iota(I32, (rows, n_win), 0) % tp
        wc = lax.broadcasted_iota(I32, (rows, n_win), 1)
        wdiff = wr + n_win - wc
        win_bias = jnp.where((wdiff >= 0) & (wdiff < WINDOW), 0.0, NEG)
        gates = g_ref[...]
        for k in range(N_KV):
            qk = q_heads(k)

            def new_rows(kind, k=k):
                c0 = COL_KV + kind * KV_WIDTH + k * HEAD_DIM
                return zs_ref[:, c0:c0 + HEAD_DIM].astype(BF16)

            _flash_update(m_sc, l_sc, acc_sc, k, 0, rows, _dot_nt(qk, new_rows(2)) + new_bias, new_rows(3))
            o_slc = acc_sc[k] * (1.0 / l_sc[k])
            kw = win_rows[pl.ds(k, n_win, stride=2 * N_KV), :].astype(BF16)
            vw = win_rows[pl.ds(N_KV + k, n_win, stride=2 * N_KV), :].astype(BF16)
            s1 = _dot_nt(qk, kw) + win_bias
            s2 = _dot_nt(qk, new_rows(4)) + new_bias
            mx = jnp.maximum(jnp.max(s1, axis=1, keepdims=True), jnp.max(s2, axis=1, keepdims=True))
            p1 = jnp.exp(s1 - mx)
            p2 = jnp.exp(s2 - mx)
            den = jnp.sum(p1, axis=1, keepdims=True) + jnp.sum(p2, axis=1, keepdims=True)
            o_win = (_dot(p1.astype(BF16), vw) + _dot(p2.astype(BF16), new_rows(5))) * (1.0 / den)
            o_cmp = ocmp_sc[k]
            for g in range(GROUP):
                r = slice(g * tp, (g + 1) * tp)
                c = k * LANES
                o = (gates[:, c + g:c + g + 1] * o_cmp[r]
                     + gates[:, c + GROUP + g:c + GROUP + g + 1] * o_slc[r]
                     + gates[:, c + 2 * GROUP + g:c + 2 * GROUP + g + 1] * o_win[r])
                h = k * GROUP + g
                o_ref[:, h * HEAD_DIM:(h + 1) * HEAD_DIM] = o.astype(o_ref.dtype)


def _attn_sample(c5, layer, page_table, zs, gates, kc, vc, win5, t_new, n_pages):
    bsz, tot_pages = page_table.shape
    n_groups = tot_pages // n_pages
    past_len = tot_pages * PAGE
    n_slc = -(-(past_len + t_new) // SLC_LEN)
    sel_lanes = -(-n_slc // LANES) * LANES
    n_win = win5.shape[2]
    tp = SUBLANES
    rows = GROUP * tp
    page_specs = [pl.BlockSpec((None, None, PAGE, 2 * N_KV, HEAD_DIM),
                               functools.partial(lambda b, g, pt, i: (layer, pt[b, g * n_pages + i], 0, 1, 0), i=i))
                  for i in range(n_pages)]

    def bmap(b, g, pt):
        return (b, 0, 0)

    grid_spec = pltpu.PrefetchScalarGridSpec(
        num_scalar_prefetch=1,
        grid=(bsz, n_groups),
        in_specs=page_specs + [
            pl.BlockSpec((None, tp, zs.shape[2]), bmap),
            pl.BlockSpec((None, kc.shape[1], KV_WIDTH), bmap),
            pl.BlockSpec((None, vc.shape[1], KV_WIDTH), bmap),
            pl.BlockSpec((None, None, n_win, 2 * N_KV, HEAD_DIM), lambda b, g, pt: (layer, b, 0, 0, 0)),
            pl.BlockSpec((None, tp, N_KV * LANES), bmap)],
        out_specs=pl.BlockSpec((None, tp, ATTN_WIDTH), bmap),
        scratch_shapes=[pltpu.VMEM((N_KV, rows, HEAD_DIM), F32),
                        pltpu.VMEM((N_KV, tp, sel_lanes), F32),
                        pltpu.VMEM((N_KV, rows, LANES), F32),
                        pltpu.VMEM((N_KV, rows, LANES), F32),
                        pltpu.VMEM((N_KV, rows, HEAD_DIM), F32)],
    )
    return pl.pallas_call(
        functools.partial(_attn_sample_kernel, n_pages=n_pages, n_groups=n_groups, t_new=t_new,
                          n_slc=n_slc, n_win=n_win),
        out_shape=jax.ShapeDtypeStruct((bsz, tp, ATTN_WIDTH), BF16),
        grid_spec=grid_spec,
        compiler_params=_cparams(("parallel", "arbitrary")),
        name="attn_sample",
    )(page_table, *([c5] * n_pages), zs, kc, vc, win5, gates)


CONV_HALO = 32
CONV_LANE_CHUNK = 256


def _conv_kernel(*refs, tc, multi_tile):
    if multi_tile:
        u_ref, up_ref, buf_ref, dw_ref, db_ref, lg_ref, lb_ref, o_ref, tail_ref, ext_sc, y_sc = refs
    else:
        u_ref, buf_ref, dw_ref, db_ref, lg_ref, lb_ref, o_ref, tail_ref, ext_sc, y_sc = refs
    t = pl.program_id(1)

    def glu(u):
        return u[:, :CONV_CH] * jax.nn.sigmoid(u[:, CONV_CH:])

    @pl.when(t == 0)
    def _():
        ext_sc[0:CONV_HALO, :] = buf_ref[...]

    if multi_tile:
        @pl.when(t > 0)
        def _():
            ext_sc[0:CONV_HALO, :] = glu(up_ref[...])

    ext_sc[CONV_HALO:CONV_HALO + tc, :] = glu(u_ref[...])
    off = CONV_HALO - (CONV_WIDTH - 1)
    n_ext = CONV_HALO + tc
    for c0 in range(0, CONV_CH, CONV_LANE_CHUNK):
        cs = slice(c0, c0 + CONV_LANE_CHUNK)
        ext = ext_sc[:, cs]
        shifted = [ext] + [pltpu.roll(ext, n_ext - s, 0) for s in range(1, SUBLANES)]
        acc = jnp.broadcast_to(db_ref[:, cs], (tc, CONV_LANE_CHUNK))
        for w in range(CONV_WIDTH):
            s = (off + w) % SUBLANES
            a = off + w - s
            acc = acc + shifted[s][a:a + tc] * dw_ref[w:w + 1, cs]
        y_sc[:, cs] = acc
    y = y_sc[...]
    mu = jnp.mean(y, axis=-1, keepdims=True)
    d = y - mu
    var = jnp.mean(d * d, axis=-1, keepdims=True)
    yn = d * lax.rsqrt(var + LN_EPS) * lg_ref[...] + lb_ref[...]
    o_ref[...] = (yn * jax.nn.sigmoid(yn)).astype(o_ref.dtype)

    @pl.when(t == pl.num_programs(1) - 1)
    def _():
        tail_ref[...] = ext_sc[tc:tc + CONV_HALO, :]


def _conv_group(z, buf, dw_w, dw_b, ln_g, ln_b, tc):
    bsz, t_len, _ = z.shape
    n_t = t_len // tc
    multi_tile = n_t > 1
    assert COL_U == 0 and (not multi_tile or tc % CONV_HALO == 0)
    halo_per_tile = max(tc // CONV_HALO, 1)

    def vec(v):
        return v.reshape(1, CONV_CH)

    cmap = lambda b, t: (0, 0)
    in_specs = [pl.BlockSpec((None, tc, 2 * CONV_CH), lambda b, t: (b, t, 0))]
    args = [z]
    if multi_tile:
        in_specs.append(pl.BlockSpec((None, CONV_HALO, 2 * CONV_CH),
                                     lambda b, t: (b, jnp.maximum(t * halo_per_tile - 1, 0), 0)))
        args.append(z)
    in_specs += [pl.BlockSpec((None, CONV_HALO, CONV_CH), lambda b, t: (b, 0, 0)),
                 pl.BlockSpec((CONV_WIDTH, CONV_CH), cmap),
                 pl.BlockSpec((1, CONV_CH), cmap), pl.BlockSpec((1, CONV_CH), cmap),
                 pl.BlockSpec((1, CONV_CH), cmap)]
    args += [buf, dw_w, vec(dw_b), vec(ln_g), vec(ln_b)]
    return pl.pallas_call(
        functools.partial(_conv_kernel, tc=tc, multi_tile=multi_tile),
        out_shape=(jax.ShapeDtypeStruct((bsz, t_len, CONV_CH), BF16),
                   jax.ShapeDtypeStruct((bsz, CONV_HALO, CONV_CH), F32)),
        grid=(bsz, n_t),
        in_specs=in_specs,
        out_specs=(pl.BlockSpec((None, tc, CONV_CH), lambda b, t: (b, t, 0)),
                   pl.BlockSpec((None, CONV_HALO, CONV_CH), lambda b, t: (b, 0, 0))),
        scratch_shapes=[pltpu.VMEM((CONV_HALO + tc, CONV_CH), F32), pltpu.VMEM((tc, CONV_CH), F32)],
        compiler_params=_cparams(("parallel", "arbitrary")),
        name="conv_group",
    )(*args)


def _rope_tables(pos):
    half = ROT_DIM // 2
    inv = ROPE_THETA ** (-jnp.arange(half, dtype=F32) / half)
    ang = pos.astype(F32)[:, None] * inv[None, :]
    cos, sin = jnp.cos(ang), jnp.sin(ang)
    n = pos.shape[0]
    zeros = jnp.zeros((n, half), F32)
    rest = HEAD_DIM - ROT_DIM
    c = jnp.concatenate([cos, cos, jnp.ones((n, rest), F32)], axis=1)
    s1 = jnp.concatenate([zeros, sin, jnp.zeros((n, rest), F32)], axis=1)
    s2 = jnp.concatenate([-sin, zeros, jnp.zeros((n, rest), F32)], axis=1)
    return c, s1, s2


def _prep_layer(w_in_l, w_out_l, w_gate_l, w_up_l, w_down_l, cmp_pe_l, cmp_w1_l, cmp_w2_l, cmp_b2_l):
    d = w_in_l.shape[0]
    kv_end = ATTN_WIDTH + KV_COLS
    gate_end = kv_end + 3 * N_HEADS
    w_main = jnp.concatenate([w_in_l[:, gate_end:], w_in_l[:, :kv_end]], axis=1).astype(BF16)
    wg = w_in_l[:, kv_end:gate_end].reshape(d, 3, N_KV, GROUP).transpose(0, 2, 1, 3).reshape(d, N_KV, 3 * GROUP)
    wg = jnp.pad(wg, ((0, 0), (0, 0), (0, LANES - 3 * GROUP))).reshape(d, N_KV * LANES).astype(BF16)
    w_gate_b = w_gate_l.astype(BF16)
    w_up_b = w_up_l.astype(BF16)
    w_down_b = w_down_l.astype(BF16)
    pef = jnp.pad(cmp_pe_l.reshape(2, 1, CMP_LEN * HEAD_DIM), ((0, 0), (0, SUBLANES - 1), (0, 0))).astype(BF16)
    w1 = cmp_w1_l.astype(BF16).reshape(2, CMP_LEN // CMP_STRIDE, CMP_STRIDE * HEAD_DIM, CMP_HIDDEN)
    cw = (pef, w1, cmp_w2_l.astype(BF16), cmp_b2_l.reshape(2, 1, HEAD_DIM))
    return w_main, wg, w_out_l.astype(BF16), w_gate_b, w_up_b, w_down_b, cw


def _largest_tile(n, unit, cap):
    best = None
    for t in range(unit, min(n, cap) + 1, unit):
        if n % t == 0:
            best = t
    assert best is not None, (n, unit, cap)
    return best


def _dense_tail(x, attn, conv, w_out_b, norm_ffn_l, w_gate_b, w_up_b, w_down_b, tm):
    d = x.shape[1]
    d_ff = w_gate_b.shape[1]
    h = _outproj(attn, conv, w_out_b, x, tm, 512)
    hn = _rmsnorm(h, norm_ffn_l, BF16)
    act = _ffn_up(hn, w_gate_b, w_up_b, tm, _largest_tile(d_ff, 2 * LANES, 512))
    return _ffn_down(act, w_down_b, h, min(tm, 512), min(d, 1024), _largest_tile(d_ff, LANES, 6144))


def kernel(x_prompt, x_sample, cache_kv, cache_win, state_conv, page_table, norm_mix, w_in, cmp_pe, cmp_w1,
           cmp_w2, cmp_b2, conv_dw_w, conv_dw_b, conv_ln_g, conv_ln_b, w_out, norm_ffn, w_gate, w_up, w_down,
           norm_final):
    depth = w_in.shape[0]
    bsz, seq, d = x_prompt.shape
    dec_b, dec_t, _ = x_sample.shape
    n_pool, page_size = cache_kv.shape[1], cache_kv.shape[2]
    tot_pages = page_table.shape[1]
    past_len = tot_pages * page_size
    assert page_size == PAGE and seq % PAGE == 0 and dec_t <= SUBLANES
    assert conv_dw_w.shape[2] == CONV_CH and d == ATTN_WIDTH + CONV_CH
    pages_per_group = min(16, tot_pages, seq // PAGE)
    assert tot_pages % pages_per_group == 0 and (seq // PAGE) % pages_per_group == 0
    tp = SUBLANES
    ms = dec_b * tp

    tm_p = min(512, seq)
    tabs_p = _rope_tables(jnp.arange(seq, dtype=I32))
    tabs_s = tuple(jnp.tile(t, (dec_b, 1)) for t in _rope_tables(past_len + jnp.arange(tp, dtype=I32)))

    hp = x_prompt.reshape(bsz * seq, d)
    hs = jnp.pad(x_sample, ((0, 0), (0, tp - dec_t), (0, 0))).reshape(ms, d)
    cache5 = cache_kv.reshape(depth, n_pool, page_size, 4 * N_KV, HEAD_DIM)
    win5 = cache_win.reshape(depth, dec_b, cache_win.shape[2], 2 * N_KV, HEAD_DIM)
    zero_buf = jnp.zeros((bsz, CONV_HALO, CONV_CH), F32)
    tm_dense = min(1024, bsz * seq)

    z_layers, convp, kvs, wins, convs = [], [], [], [], []
    for l in range(depth):
        w_main, wg, w_out_b, w_gate_b, w_up_b, w_down_b, cw = _prep_layer(
            w_in[l], w_out[l], w_gate[l], w_up[l], w_down[l], cmp_pe[l], cmp_w1[l], cmp_w2[l], cmp_b2[l])

        z, gates = _inproj(hp, norm_mix[l], w_main, wg, tabs_p, seq // tm_p, tm_p)
        z_layers.append(z)
        z3 = z.reshape(bsz, seq, z.shape[1])
        kc, vc = _compress_prompt(z3, cw, pages_per_group)
        attn = _attn_prompt(z3, gates.reshape(bsz, seq, N_KV * LANES), kc, vc)
        conv, tail = _conv_group(z3, zero_buf, conv_dw_w[l], conv_dw_b[l], conv_ln_g[l], conv_ln_b[l], PAGE)
        hp = _dense_tail(hp, attn.reshape(bsz * seq, ATTN_WIDTH), conv.reshape(bsz * seq, CONV_CH), w_out_b,
                         norm_ffn[l], w_gate_b, w_up_b, w_down_b, tm_dense)
        convp.append(tail[:, CONV_HALO - (CONV_WIDTH - 1):])

        zs, gates_s = _inproj(hs, norm_mix[l], w_main, wg, tabs_s, 1, ms)
        zs3 = zs.reshape(dec_b, tp, zs.shape[1])
        kc_s, vc_s = _compress_sample(cache5, l, page_table, cw, pages_per_group)
        attn_s = _attn_sample(cache5, l, page_table, zs3, gates_s.reshape(dec_b, tp, N_KV * LANES),
                              kc_s, vc_s, win5, dec_t, pages_per_group)
        buf_s = jnp.pad(state_conv[l], ((0, 0), (CONV_HALO - (CONV_WIDTH - 1), 0), (0, 0)))
        conv_s, tail_s = _conv_group(zs3, buf_s, conv_dw_w[l], conv_dw_b[l], conv_ln_g[l], conv_ln_b[l], tp)
        hs = _dense_tail(hs, attn_s.reshape(ms, ATTN_WIDTH), conv_s.reshape(ms, CONV_CH), w_out_b, norm_ffn[l],
                         w_gate_b, w_up_b, w_down_b, ms)
        kvs.append(zs3[:, :dec_t, COL_KV:COL_KV + 4 * KV_WIDTH].reshape(dec_b, dec_t, 4, N_KV, HEAD_DIM))
        new_win_s = zs3[:, :dec_t, COL_KV + 4 * KV_WIDTH:COL_KV + KV_COLS].reshape(dec_b, dec_t, 2, N_KV, HEAD_DIM)
        win_all = jnp.concatenate([cache_win[l], new_win_s], axis=1)
        wins.append(win_all[:, win_all.shape[1] - min(WINDOW, win_all.shape[1]):])
        new_glu = tail_s[:, CONV_HALO - tp:CONV_HALO - tp + dec_t]
        ext_s = jnp.concatenate([state_conv[l], new_glu], axis=1)
        convs.append(ext_s[:, ext_s.shape[1] - (CONV_WIDTH - 1):])

    y_prompt = _rmsnorm(hp, norm_final, F32).reshape(bsz, seq, d)
    y_sample = _rmsnorm(hs, norm_final, F32).reshape(dec_b, tp, d)[:, :dec_t]
    keep = min(WINDOW, seq)
    tiles_per_batch = seq // keep
    assert seq % keep == 0
    kv_rows = _rows_out(z_layers, bsz * tiles_per_batch, keep, COL_KV, 4 * N_KV, lambda i: i)
    win_rows = _rows_out(z_layers, bsz, keep, COL_KV + 4 * KV_WIDTH, 2 * N_KV,
                         lambda i: i * tiles_per_batch + tiles_per_batch - 1)
    kv_prompt = kv_rows.reshape(depth, bsz, seq, 4, N_KV, HEAD_DIM)
    win_prompt = win_rows.reshape(depth, bsz, keep, 2, N_KV, HEAD_DIM)
    return (y_prompt, y_sample, kv_prompt, win_prompt, jnp.stack(convp),
            jnp.stack(kvs), jnp.stack(wins), jnp.stack(convs))
```

```python
import functools
import math

import jax
import jax.numpy as jnp
from jax import lax
from jax.experimental import pallas as pl
from jax.experimental.pallas import tpu as pltpu

F32 = jnp.float32
BF16 = jnp.bfloat16
I32 = jnp.int32

N_HEADS = 16
N_KV = 4
HEAD_DIM = 128
GROUP = N_HEADS // N_KV
ATTN_WIDTH = N_HEADS * HEAD_DIM
KV_WIDTH = N_KV * HEAD_DIM
CONV_CH = 2048
CONV_WIDTH = 31
ROT_DIM = HEAD_DIM // 4
ROPE_THETA = 500000.0
CMP_LEN = 32
CMP_STRIDE = 16
CMP_HIDDEN = 2 * HEAD_DIM
SLC_LEN = 64
SLC_SHIFT = 6
N_SEL = 16
WINDOW = 512
SCALE = HEAD_DIM ** -0.5
LOG2E = 1.0 / math.log(2.0)
SLC_KEY_BLOCK = 512
NEG = -1e30
FORCE = 1e9
RMS_EPS = 1e-6
LN_EPS = 1e-5

LANES = 128
SUBLANES = 8
VMEM_LIMIT = 56 * 1024 * 1024

PAGE = 128
CHUNKS_PER_PAGE = PAGE // CMP_STRIDE
COL_U = 0
COL_Q = 2 * CONV_CH
COL_KV = COL_Q + ATTN_WIDTH
KV_COLS = 6 * KV_WIDTH
PROJ_TN = 512


def _cparams(sem):
    return pltpu.CompilerParams(dimension_semantics=sem, vmem_limit_bytes=VMEM_LIMIT)


def _dot(a, b):
    return jnp.dot(a, b, preferred_element_type=F32)


def _dot_nt(a, b):
    return lax.dot_general(a, b, (((1,), (1,)), ((), ())), preferred_element_type=F32)


def _split3_dot(x, m_bf16):
    hi = x.astype(BF16)
    r1 = x - hi.astype(F32)
    mid = r1.astype(BF16)
    lo = (r1 - mid.astype(F32)).astype(BF16)
    return _dot(hi, m_bf16) + _dot(mid, m_bf16) + _dot(lo, m_bf16)


def _rmsnorm_kernel(x_ref, g_ref, o_ref):
    x = x_ref[...]
    ms = jnp.mean(x * x, axis=-1, keepdims=True)
    o_ref[...] = (x * lax.rsqrt(ms + RMS_EPS) * g_ref[...]).astype(o_ref.dtype)


def _rmsnorm(x, g, out_dtype):
    m, d = x.shape
    tm = min(m, 256)
    return pl.pallas_call(
        _rmsnorm_kernel,
        out_shape=jax.ShapeDtypeStruct((m, d), out_dtype),
        grid=(m // tm,),
        in_specs=[pl.BlockSpec((tm, d), lambda i: (i, 0)),
                  pl.BlockSpec((1, d), lambda i: (0, 0))],
        out_specs=pl.BlockSpec((tm, d), lambda i: (i, 0)),
        compiler_params=_cparams(("parallel",)),
        name="rmsnorm",
    )(x, g.reshape(1, d))


def _inproj_kernel(x_ref, gn_ref, w_ref, wg_ref, c_ref, s1_ref, s2_ref, z_ref, g_ref, zc_ref, xn_sc):
    j = pl.program_id(1)
    n_heads_tile = PROJ_TN // HEAD_DIM

    @pl.when(j == 0)
    def _():
        x = x_ref[...]
        ms = jnp.mean(x * x, axis=-1, keepdims=True)
        xn_sc[...] = (x * lax.rsqrt(ms + RMS_EPS) * gn_ref[...]).astype(xn_sc.dtype)

    a = xn_sc[...]
    acc = _dot(a, w_ref[...])
    q0 = COL_Q // PROJ_TN
    k0 = COL_KV // PROJ_TN
    is_rope = ((j >= q0) & (j <= k0)) | (j == k0 + 2) | (j == k0 + 4)

    @pl.when(is_rope)
    def _():
        c = c_ref[...]
        s1 = s1_ref[...]
        s2 = s2_ref[...]
        rotated = []
        for h in range(n_heads_tile):
            x = acc[:, h * HEAD_DIM:(h + 1) * HEAD_DIM]
            rotated.append(x * c + pltpu.roll(x, ROT_DIM // 2, 1) * s1
                           + pltpu.roll(x, HEAD_DIM - ROT_DIM // 2, 1) * s2)
            z_ref[:, h * HEAD_DIM:(h + 1) * HEAD_DIM] = rotated[h]

        @pl.when(j == k0)
        def _():
            for h in range(n_heads_tile):
                zc_ref[:, h, :] = rotated[h]

    @pl.when(jnp.logical_not(is_rope))
    def _():
        z_ref[...] = acc

        @pl.when(j == k0 + 1)
        def _():
            for h in range(n_heads_tile):
                zc_ref[:, n_heads_tile + h, :] = acc[:, h * HEAD_DIM:(h + 1) * HEAD_DIM]

    @pl.when(j == 0)
    def _():
        g_ref[...] = jax.nn.sigmoid(_dot(a, wg_ref[...]))


def _inproj(x, g_norm, w, wg, layer, tabs, tab_tiles, tm):
    m, k = x.shape
    n = w.shape[2]
    c, s1, s2 = tabs
    tab_spec = pl.BlockSpec((tm, HEAD_DIM), lambda i, j: (i % tab_tiles, 0))
    return pl.pallas_call(
        _inproj_kernel,
        out_shape=(jax.ShapeDtypeStruct((m, n), F32),
                   jax.ShapeDtypeStruct((m, N_KV * LANES), F32),
                   jax.ShapeDtypeStruct((m, 2 * N_KV, HEAD_DIM), F32)),
        grid=(m // tm, n // PROJ_TN),
        in_specs=[pl.BlockSpec((tm, k), lambda i, j: (i, 0)),
                  pl.BlockSpec((1, k), lambda i, j: (0, 0)),
                  pl.BlockSpec((None, k, PROJ_TN), lambda i, j: (layer, 0, j)),
                  pl.BlockSpec((None, k, N_KV * LANES), lambda i, j: (layer, 0, 0)),
                  tab_spec, tab_spec, tab_spec],
        out_specs=(pl.BlockSpec((tm, PROJ_TN), lambda i, j: (i, j)),
                   pl.BlockSpec((tm, N_KV * LANES), lambda i, j: (i, 0)),
                   pl.BlockSpec((tm, 2 * N_KV, HEAD_DIM), lambda i, j: (i, 0, 0))),
        scratch_shapes=[pltpu.VMEM((tm, k), BF16)],
        compiler_params=_cparams(("parallel", "arbitrary")),
        name="inproj",
    )(x, g_norm.reshape(1, k), w, wg, c, s1, s2)


def _rows_out_kernel(*refs, depth):
    z_refs = refs[:depth]
    o_ref = refs[depth]
    layer = pl.program_id(0)
    for li, z_ref in enumerate(z_refs):
        @pl.when(layer == li)
        def _(z_ref=z_ref):
            for r in range(o_ref.shape[1]):
                o_ref[:, r, :] = z_ref[:, r * HEAD_DIM:(r + 1) * HEAD_DIM]


def _rows_out(zs, n_tiles, tm, col0, n_head_rows, row_tile_of):
    depth = len(zs)
    width = n_head_rows * HEAD_DIM
    assert col0 % width == 0
    in_specs = [pl.BlockSpec((tm, width),
                             functools.partial(lambda l, i, li: (jnp.where(l == li, row_tile_of(i), 0), col0 // width),
                                               li=li))
                for li in range(depth)]
    return pl.pallas_call(
        functools.partial(_rows_out_kernel, depth=depth),
        out_shape=jax.ShapeDtypeStruct((depth, n_tiles * tm, n_head_rows, HEAD_DIM), F32),
        grid=(depth, n_tiles),
        in_specs=in_specs,
        out_specs=pl.BlockSpec((None, tm, n_head_rows, HEAD_DIM), lambda l, i: (l, i, 0, 0)),
        compiler_params=_cparams(("arbitrary", "arbitrary")),
        name="rows_out",
    )(*zs)


def _outproj_kernel(a1_ref, a2_ref, w1_ref, w2_ref, r_ref, o_ref):
    o_ref[...] = r_ref[...] + _dot(a1_ref[...], w1_ref[...]) + _dot(a2_ref[...], w2_ref[...])


def _outproj(a1, a2, w, layer, res, tm, tn):
    m, k1 = a1.shape
    k2 = a2.shape[1]
    n = w.shape[2]
    assert k1 == k2 and w.shape[1] == k1 + k2
    return pl.pallas_call(
        _outproj_kernel,
        out_shape=jax.ShapeDtypeStruct((m, n), F32),
        grid=(m // tm, n // tn),
        in_specs=[pl.BlockSpec((tm, k1), lambda i, j: (i, 0)),
                  pl.BlockSpec((tm, k2), lambda i, j: (i, 0)),
                  pl.BlockSpec((None, k1, tn), lambda i, j: (layer, 0, j)),
                  pl.BlockSpec((None, k2, tn), lambda i, j: (layer, 1, j)),
                  pl.BlockSpec((tm, tn), lambda i, j: (i, j))],
        out_specs=pl.BlockSpec((tm, tn), lambda i, j: (i, j)),
        compiler_params=_cparams(("parallel", "arbitrary")),
        name="outproj",
    )(a1, a2, w, w, res)


def _ffn_up_kernel(a_ref, wg_ref, wu_ref, o_ref):
    a = a_ref[...]
    gate = _dot(a, wg_ref[...])
    up = _dot(a, wu_ref[...])
    o_ref[...] = (gate * jax.nn.sigmoid(gate) * up).astype(o_ref.dtype)


def _ffn_up(a, wg, wu, layer, tm, tn):
    m, k = a.shape
    n = wg.shape[2]
    return pl.pallas_call(
        _ffn_up_kernel,
        out_shape=jax.ShapeDtypeStruct((m, n), BF16),
        grid=(m // tm, n // tn),
        in_specs=[pl.BlockSpec((tm, k), lambda i, j: (i, 0)),
                  pl.BlockSpec((None, k, tn), lambda i, j: (layer, 0, j)),
                  pl.BlockSpec((None, k, tn), lambda i, j: (layer, 0, j))],
        out_specs=pl.BlockSpec((tm, tn), lambda i, j: (i, j)),
        compiler_params=_cparams(("parallel", "arbitrary")),
        name="ffn_up",
    )(a, wg, wu)


def _ffn_down_kernel(a_ref, w_ref, r_ref, o_ref, acc_ref):
    kk = pl.program_id(2)

    @pl.when(kk == 0)
    def _():
        acc_ref[...] = r_ref[...]

    acc_ref[...] += _dot(a_ref[...], w_ref[...])

    @pl.when(kk == pl.num_programs(2) - 1)
    def _():
        o_ref[...] = acc_ref[...]


def _ffn_down(a, w, layer, res, tm, tn, tk):
    m, k = a.shape
    n = w.shape[2]
    return pl.pallas_call(
        _ffn_down_kernel,
        out_shape=jax.ShapeDtypeStruct((m, n), F32),
        grid=(m // tm, n // tn, k // tk),
        in_specs=[pl.BlockSpec((tm, tk), lambda i, j, kk: (i, kk)),
                  pl.BlockSpec((None, tk, tn), lambda i, j, kk: (layer, kk, j)),
                  pl.BlockSpec((tm, tn), lambda i, j, kk: (i, j))],
        out_specs=pl.BlockSpec((tm, tn), lambda i, j, kk: (i, j)),
        scratch_shapes=[pltpu.VMEM((tm, tn), F32)],
        compiler_params=_cparams(("parallel", "parallel", "arbitrary")),
        name="ffn_down",
    )(a, w, res)


def _gelu_tanh(x):
    return 0.5 * x * (1.0 + jnp.tanh(math.sqrt(2.0 / math.pi) * (x + 0.044715 * (x * x * x))))


def _compress_kernel(*refs, n_pages, n_prefetch):
    page_refs = refs[n_prefetch:n_prefetch + n_pages]
    pef_ref, w1_ref, w2_ref, b2_ref, kc_ref, vc_ref, carry_ref = refs[n_prefetch + n_pages:]
    grp = pl.program_id(1)
    n_chunks = n_pages * CHUNKS_PER_PAGE
    half_k = CMP_STRIDE * HEAD_DIM

    @pl.when(grp == 0)
    def _():
        carry_ref[...] = jnp.zeros_like(carry_ref)

    rows = N_KV * n_chunks
    pairs = CHUNKS_PER_PAGE // 2
    lo = lax.broadcasted_iota(I32, (pairs, CMP_STRIDE, 2 * N_KV, HEAD_DIM), 2) < N_KV
    xk, xv = [], []
    for p in page_refs:
        x = p[...].reshape(pairs, 2, CMP_STRIDE, 2 * N_KV, HEAD_DIM)
        even, odd = x[:, 0], x[:, 1]
        kp = jnp.where(lo, even, pltpu.roll(odd, N_KV, 2))
        vp = jnp.where(lo, pltpu.roll(even, N_KV, 2), odd)
        for t, dst in ((kp, xk), (vp, xv)):
            dst.append(jnp.concatenate([t[:, j] for j in range(CMP_STRIDE)], axis=-1)
                       .reshape(pairs * 2 * N_KV, CMP_STRIDE * HEAD_DIM))
    row = lax.broadcasted_iota(I32, (rows, CMP_HIDDEN), 0)
    for kind, parts, out_ref in ((0, xk, kc_ref), (1, xv, vc_ref)):
        x_all = jnp.concatenate(parts, axis=0).astype(BF16)
        a0 = _dot(x_all, w1_ref[kind, 0])
        a1 = _dot(x_all, w1_ref[kind, 1])
        pe = pef_ref[kind]
        pe_term = (_dot(pe[:, :half_k], w1_ref[kind, 0]) + _dot(pe[:, half_k:], w1_ref[kind, 1]))[0:1, :]
        shifted = pltpu.roll(jnp.where(row >= rows - N_KV, carry_ref[kind], a0), N_KV, 0)
        pre = shifted + a1 + pe_term
        out_ref[...] = (_dot(_gelu_tanh(pre).astype(BF16), w2_ref[kind]) + b2_ref[kind]).astype(out_ref.dtype)
        carry_ref[kind] = a0


def _compress_prompt(zc, cw, n_pages):
    bsz, t_len = zc.shape[:2]
    specs = [pl.BlockSpec((None, PAGE, 2 * N_KV, HEAD_DIM),
                          functools.partial(lambda b, g, i: (b, g * n_pages + i, 0, 0), i=i))
             for i in range(n_pages)]
    return _compress(zc, specs, bsz, (t_len // PAGE) // n_pages, n_pages, cw)


def _compress_sample(c5, layer, page_table, cw, n_pages):
    bsz, tot_pages = page_table.shape
    specs = [pl.BlockSpec((None, None, PAGE, 2 * N_KV, HEAD_DIM),
                          functools.partial(lambda b, g, pt, i: (layer, pt[b, g * n_pages + i], 0, 0, 0), i=i))
             for i in range(n_pages)]
    return _compress(c5, specs, bsz, tot_pages // n_pages, n_pages, cw, prefetch=(page_table,))


def _compress(src, page_specs, n_batch, n_groups, n_pages, cw, prefetch=()):
    pef, w1, w2, b2 = cw
    n_chunks = n_pages * CHUNKS_PER_PAGE

    def const_spec(x):
        return pl.BlockSpec(x.shape, lambda b, g, *pt: (0,) * x.ndim)

    out_spec = pl.BlockSpec((None, N_KV * n_chunks, HEAD_DIM), lambda b, g, *pt: (b, g, 0))
    out_shape = jax.ShapeDtypeStruct((n_batch, n_groups * N_KV * n_chunks, HEAD_DIM), BF16)
    grid_spec = pltpu.PrefetchScalarGridSpec(
        num_scalar_prefetch=len(prefetch),
        grid=(n_batch, n_groups),
        in_specs=list(page_specs) + [const_spec(pef), const_spec(w1), const_spec(w2), const_spec(b2)],
        out_specs=(out_spec, out_spec),
        scratch_shapes=[pltpu.VMEM((2, N_KV * n_chunks, CMP_HIDDEN), F32)],
    )
    kc, vc = pl.pallas_call(
        functools.partial(_compress_kernel, n_pages=n_pages, n_prefetch=len(prefetch)),
        out_shape=(out_shape, out_shape),
        grid_spec=grid_spec,
        compiler_params=_cparams(("parallel", "arbitrary")),
        name="compress",
    )(*prefetch, *([src] * n_pages), pef, w1, w2, b2)
    return (kc.reshape(n_batch, n_groups * n_chunks, KV_WIDTH), vc.reshape(n_batch, n_groups * n_chunks, KV_WIDTH))


def _flash_init(m_sc, l_sc, acc_sc, br):
    m_sc[br] = jnp.full(m_sc.shape[1:], -jnp.inf, F32)
    l_sc[br] = jnp.zeros(l_sc.shape[1:], F32)
    acc_sc[br] = jnp.zeros(acc_sc.shape[1:], F32)


def _flash_update(m_sc, l_sc, acc_sc, br, r0, nr, s, v):
    m_old = m_sc[br, r0:r0 + nr, :]
    m_new = jnp.maximum(m_old, jnp.max(s, axis=1, keepdims=True))
    alpha = jnp.exp(m_old - m_new)
    p = jnp.exp(s - (m_new if s.shape[1] == LANES else m_new[:, 0:1]))
    l_sc[br, r0:r0 + nr, :] = alpha * l_sc[br, r0:r0 + nr, :] + jnp.sum(p, axis=1, keepdims=True)
    acc_sc[br, r0:r0 + nr, :] = alpha * acc_sc[br, r0:r0 + nr, :] + _dot(p.astype(BF16), v)
    m_sc[br, r0:r0 + nr, :] = m_new


def _overlap_matrix(shape, n_slc):
    m = lax.broadcasted_iota(I32, shape, 0)
    j = lax.broadcasted_iota(I32, shape, 1)
    start = (m - 1) * CMP_STRIDE
    ov = (m >= 1) & (start < j * SLC_LEN + SLC_LEN) & (start + CMP_LEN > j * SLC_LEN) & (j < n_slc)
    return jnp.where(ov, 1.0, 0.0).astype(BF16)


def _lane_tile_reduce(x, op):
    out = x[..., 0:LANES]
    for i in range(1, x.shape[-1] // LANES):
        out = op(out, x[..., i * LANES:(i + 1) * LANES])
    return out


def _attn_prompt_kernel(q_ref, kc_ref, vc_ref, ks_ref, vs_ref, kw_ref, vw_ref, g_ref, o_ref,
                        ksb_sc, vsb_sc, kwb_sc, vwb_sc, s_sc, m_sc, l_sc, acc_sc, *, n_slc, kb, win_keys):
    qt = pl.program_id(2)
    tq = PAGE
    rows = GROUP * tq
    t_len = ks_ref.shape[0]

    @pl.when(qt == 0)
    def _():
        ksb_sc[...] = ks_ref[...].astype(BF16)
        vsb_sc[...] = vs_ref[...].astype(BF16)
        kwb_sc[...] = kw_ref[...].astype(BF16)
        vwb_sc[...] = vw_ref[...].astype(BF16)

    q = q_ref[...] * (SCALE * LOG2E)
    q_all = jnp.concatenate([q[:, g * HEAD_DIM:(g + 1) * HEAD_DIM] for g in range(GROUP)],
                            axis=0).astype(BF16)
    row = lax.broadcasted_iota(I32, (tq, LANES), 0)
    lane = lax.broadcasted_iota(I32, (tq, LANES), 1)
    pos = qt * tq + row

    valid = (lane >= 1) & (lane * CMP_STRIDE + (CMP_LEN - CMP_STRIDE - 1) <= pos)
    vbias = jnp.where(valid, 0.0, NEG)
    vf = jnp.where(valid, 1.0, 0.0)
    s3 = _dot_nt(q_all, kc_ref[...]).reshape(GROUP, tq, LANES) + vbias[None]
    e = jnp.exp2(s3 - jnp.max(s3, axis=2, keepdims=True)) * vf[None]
    denom = jnp.sum(e, axis=2, keepdims=True)
    p3 = e * (1.0 / jnp.maximum(denom, 1e-30))
    o_cmp = _dot(p3.reshape(rows, LANES).astype(BF16), vc_ref[...])
    psum = p3[0] + p3[1] + p3[2] + p3[3]

    imp = _split3_dot(psum, _overlap_matrix((LANES, LANES), n_slc))
    cur = pos >> SLC_SHIFT
    forced = (lane == 0) | (lane == cur) | (lane == cur - 1)
    score = jnp.where(lane <= cur, jnp.where(forced, FORCE, imp), -1.0)
    score = jnp.where(lane < n_slc, score, -2.0)
    n_rank = -(-n_slc // SUBLANES) * SUBLANES
    score_t = score.T[0:n_rank]
    jrow = lax.broadcasted_iota(I32, (n_rank, tq), 0)
    rank_t = jnp.zeros((n_rank, tq), F32)
    for jp in range(n_slc):
        other = score_t[jp:jp + 1, :]
        tie = jnp.where(jrow > jp, 1.0, 0.0)
        rank_t = rank_t + jnp.where(other > score_t, 1.0, jnp.where(other == score_t, tie, 0.0))
    sel_t = jnp.where((rank_t < min(N_SEL, n_slc)) & (jrow < n_slc), 1.0, 0.0)
    sel = jnp.concatenate([sel_t, jnp.zeros((LANES - n_rank, tq), F32)], axis=0).T.astype(BF16)

    m_sc[...] = jnp.full(m_sc.shape, -jnp.inf, F32)
    l_sc[...] = jnp.zeros(l_sc.shape, F32)
    acc_sc[...] = jnp.zeros(acc_sc.shape, F32)
    n_kb = t_len // kb
    brow = lax.broadcasted_iota(I32, (LANES, kb), 0)
    blane = lax.broadcasted_iota(I32, (tq, kb), 1)
    bpos = qt * tq + lax.broadcasted_iota(I32, (tq, kb), 0)
    for cb in range(n_kb):
        @pl.when(cb * kb <= qt * tq)
        def _(cb=cb):
            key = cb * kb + blane
            expand = jnp.where(((cb * kb + lax.broadcasted_iota(I32, (LANES, kb), 1)) >> SLC_SHIFT) == brow,
                               1.0, 0.0).astype(BF16)
            selm = _dot(sel, expand)
            bias = jnp.where((selm > 0.5) & (key <= bpos), 0.0, NEG)
            s = _dot_nt(q_all, ksb_sc[cb * kb:(cb + 1) * kb, :]).reshape(GROUP, tq, kb) + bias[None]
            s_sc[:, cb * kb:(cb + 1) * kb] = s.reshape(rows, kb)
            m_sc[...] = jnp.maximum(m_sc[...], _lane_tile_reduce(s, jnp.maximum).reshape(rows, LANES))

    m_row = jnp.max(m_sc[...], axis=1, keepdims=True)
    for cb in range(n_kb):
        @pl.when(cb * kb <= qt * tq)
        def _(cb=cb):
            p = jnp.exp2(s_sc[:, cb * kb:(cb + 1) * kb] - m_row)
            l_sc[...] += _lane_tile_reduce(p, jnp.add)
            acc_sc[...] += _dot(p.astype(BF16), vsb_sc[cb * kb:(cb + 1) * kb, :])

    o_slc = acc_sc[...] * (1.0 / jnp.sum(l_sc[...], axis=1, keepdims=True))

    start = pl.multiple_of(jnp.maximum(qt * tq + tq - win_keys, 0), PAGE)
    wlane = lax.broadcasted_iota(I32, (tq, win_keys), 1)
    wdiff = (qt * tq + lax.broadcasted_iota(I32, (tq, win_keys), 0)) - (start + wlane)
    wbias = jnp.where((wdiff >= 0) & (wdiff < WINDOW), 0.0, NEG)
    sw = _dot_nt(q_all, kwb_sc[pl.ds(start, win_keys), :]).reshape(GROUP, tq, win_keys) + wbias[None]
    pw = jnp.exp2(sw - jnp.max(sw, axis=2, keepdims=True))
    lw = jnp.sum(pw, axis=2, keepdims=True).reshape(rows, 1)
    o_win = _dot(pw.reshape(rows, win_keys).astype(BF16), vwb_sc[pl.ds(start, win_keys), :]) * (1.0 / lw)

    gates = g_ref[...]
    for g in range(GROUP):
        r = slice(g * tq, (g + 1) * tq)
        o = (gates[:, g:g + 1] * o_cmp[r] + gates[:, GROUP + g:GROUP + g + 1] * o_slc[r]
             + gates[:, 2 * GROUP + g:2 * GROUP + g + 1] * o_win[r])
        o_ref[:, g * HEAD_DIM:(g + 1) * HEAD_DIM] = o.astype(o_ref.dtype)


def _attn_prompt(z, gates, kc, vc):
    bsz, t_len, _ = z.shape
    n_slc = -(-t_len // SLC_LEN)
    assert t_len % PAGE == 0 and n_slc <= LANES and kc.shape[1] <= LANES
    n_cmp_rows = kc.shape[1]
    assert n_cmp_rows == LANES, "compressed keys are laid out on one 128-lane tile"
    kv0 = COL_KV // HEAD_DIM

    def kv_spec(kind):
        return pl.BlockSpec((None, t_len, HEAD_DIM), lambda b, k, t: (b, 0, kv0 + kind * N_KV + k))

    cmp_spec = pl.BlockSpec((None, n_cmp_rows, HEAD_DIM), lambda b, k, t: (b, 0, k))
    q_blk0 = COL_Q // (GROUP * HEAD_DIM)
    kb = min(SLC_KEY_BLOCK, t_len)
    win_keys = min(WINDOW + PAGE, t_len)
    assert t_len % kb == 0
    rows = GROUP * PAGE
    kv_scratch = pltpu.VMEM((t_len, HEAD_DIM), BF16)
    return pl.pallas_call(
        functools.partial(_attn_prompt_kernel, n_slc=n_slc, kb=kb, win_keys=win_keys),
        out_shape=jax.ShapeDtypeStruct((bsz, t_len, ATTN_WIDTH), BF16),
        grid=(bsz, N_KV, t_len // PAGE),
        in_specs=[pl.BlockSpec((None, PAGE, GROUP * HEAD_DIM), lambda b, k, t: (b, t, q_blk0 + k)),
                  cmp_spec, cmp_spec,
                  kv_spec(2), kv_spec(3), kv_spec(4), kv_spec(5),
                  pl.BlockSpec((None, PAGE, LANES), lambda b, k, t: (b, t, k))],
        out_specs=pl.BlockSpec((None, PAGE, GROUP * HEAD_DIM), lambda b, k, t: (b, t, k)),
        scratch_shapes=[kv_scratch, kv_scratch, kv_scratch, kv_scratch,
                        pltpu.VMEM((rows, t_len), F32),
                        pltpu.VMEM((rows, LANES), F32),
                        pltpu.VMEM((rows, LANES), F32),
                        pltpu.VMEM((rows, HEAD_DIM), F32)],
        compiler_params=_cparams(("parallel", "parallel", "arbitrary")),
        name="attn_prompt",
    )(z, kc, vc, z, z, z, z, gates)


def _attn_sample_kernel(*refs, n_pages, n_groups, t_new, n_slc, n_win):
    page_refs = refs[1:1 + n_pages]
    zs_ref, kc_ref, vc_ref, win_ref, g_ref, o_ref, ocmp_sc, sel_sc, m_sc, l_sc, acc_sc = refs[1 + n_pages:]
    pages = [p.reshape(PAGE * 2 * N_KV, HEAD_DIM) for p in page_refs]
    win_rows = win_ref.reshape(n_win * 2 * N_KV, HEAD_DIM)
    grp = pl.program_id(1)
    tp = SUBLANES
    rows = GROUP * tp
    n_cmp_rows = kc_ref.shape[0]
    sel_lanes = sel_sc.shape[2]
    past_len = n_groups * n_pages * PAGE

    def q_heads(k):
        c0 = COL_Q + k * GROUP * HEAD_DIM
        q = zs_ref[:, c0:c0 + GROUP * HEAD_DIM] * SCALE
        return jnp.concatenate([q[:, g * HEAD_DIM:(g + 1) * HEAD_DIM] for g in range(GROUP)],
                               axis=0).astype(BF16)

    @pl.when(grp == 0)
    def _():
        ovb = _overlap_matrix((n_cmp_rows, sel_lanes), n_slc)
        mlane = lax.broadcasted_iota(I32, (rows, n_cmp_rows), 1)
        vbias = jnp.where(mlane >= 1, 0.0, NEG)
        lane = lax.broadcasted_iota(I32, (tp, sel_lanes), 1)
        lane_f = lane.astype(F32)
        tok = lax.broadcasted_iota(I32, (tp, sel_lanes), 0)
        cur = (past_len + tok) >> SLC_SHIFT
        forced = (lane == 0) | (lane == cur) | (lane == cur - 1)
        for k in range(N_KV):
            qk = q_heads(k)
            kc = kc_ref[:, k * HEAD_DIM:(k + 1) * HEAD_DIM]
            vc = vc_ref[:, k * HEAD_DIM:(k + 1) * HEAD_DIM]
            s = _dot_nt(qk, kc) + vbias
            e = jnp.exp(s - jnp.max(s, axis=1, keepdims=True))
            p = e * (1.0 / jnp.sum(e, axis=1, keepdims=True))
            ocmp_sc[k] = _dot(p.astype(BF16), vc)
            psum = p[0:tp]
            for g in range(1, GROUP):
                psum = psum + p[g * tp:(g + 1) * tp]
            imp = _split3_dot(psum, ovb)
            score = jnp.where(lane <= cur, jnp.where(forced, FORCE, imp), -1.0)
            score = jnp.where(lane < n_slc, score, -3e38)
            sel = jnp.zeros((tp, sel_lanes), F32)
            for _ in range(min(N_SEL, n_slc)):
                top = jnp.max(score, axis=1, keepdims=True)
                idx = jnp.min(jnp.where(score == top, lane_f, float(sel_lanes)), axis=1, keepdims=True)
                hit = lane_f == idx
                sel = jnp.where(hit, 1.0, sel)
                score = jnp.where(hit, -3e38, score)
            sel_sc[k] = sel
            _flash_init(m_sc, l_sc, acc_sc, k)

    nk = n_pages * PAGE
    jrow = lax.broadcasted_iota(I32, (sel_lanes, nk), 0)
    klane = lax.broadcasted_iota(I32, (sel_lanes, nk), 1)
    expand = jnp.where(jrow == grp * (nk // SLC_LEN) + (klane >> SLC_SHIFT), 1.0, 0.0).astype(BF16)
    for k in range(N_KV):
        kk = jnp.concatenate([p[pl.ds(k, PAGE, stride=2 * N_KV), :].astype(BF16) for p in pages], axis=0)
        vv = jnp.concatenate([p[pl.ds(N_KV + k, PAGE, stride=2 * N_KV), :].astype(BF16) for p in pages], axis=0)
        selm = _dot(sel_sc[k].astype(BF16), expand)
        bias = jnp.where(selm > 0.5, 0.0, NEG)
        bias = jnp.concatenate([bias] * GROUP, axis=0)
        _flash_update(m_sc, l_sc, acc_sc, k, 0, rows, _dot_nt(q_heads(k), kk) + bias, vv)

    @pl.when(grp == n_groups - 1)
    def _():
        tr = lax.broadcasted_iota(I32, (rows, tp), 0) % tp
        tc = lax.broadcasted_iota(I32, (rows, tp), 1)
        new_bias = jnp.where((tc <= tr) & (tc < t_new), 0.0, NEG)
        wr = lax.broadcasted_iota(I32, (rows, n_win), 0) % tp
        wc = lax.broadcasted_iota(I32, (rows, n_win), 1)
        wdiff = wr + n_win - wc
        win_bias = jnp.where((wdiff >= 0) & (wdiff < WINDOW), 0.0, NEG)
        gates = g_ref[...]
        for k in range(N_KV):
            qk = q_heads(k)

            def new_rows(kind, k=k):
                c0 = COL_KV + kind * KV_WIDTH + k * HEAD_DIM
                return zs_ref[:, c0:c0 + HEAD_DIM].astype(BF16)

            _flash_update(m_sc, l_sc, acc_sc, k, 0, rows, _dot_nt(qk, new_rows(2)) + new_bias, new_rows(3))
            o_slc = acc_sc[k] * (1.0 / l_sc[k])
            kw = win_rows[pl.ds(k, n_win, stride=2 * N_KV), :].astype(BF16)
            vw = win_rows[pl.ds(N_KV + k, n_win, stride=2 * N_KV), :].astype(BF16)
            s1 = _dot_nt(qk, kw) + win_bias
            s2 = _dot_nt(qk, new_rows(4)) + new_bias
            mx = jnp.maximum(jnp.max(s1, axis=1, keepdims=True), jnp.max(s2, axis=1, keepdims=True))
            p1 = jnp.exp(s1 - mx)
            p2 = jnp.exp(s2 - mx)
            den = jnp.sum(p1, axis=1, keepdims=True) + jnp.sum(p2, axis=1, keepdims=True)
            o_win = (_dot(p1.astype(BF16), vw) + _dot(p2.astype(BF16), new_rows(5))) * (1.0 / den)
            o_cmp = ocmp_sc[k]
            for g in range(GROUP):
                r = slice(g * tp, (g + 1) * tp)
                c = k * LANES
                o = (gates[:, c + g:c + g + 1] * o_cmp[r]
                     + gates[:, c + GROUP + g:c + GROUP + g + 1] * o_slc[r]
                     + gates[:, c + 2 * GROUP + g:c + 2 * GROUP + g + 1] * o_win[r])
                h = k * GROUP + g
                o_ref[:, h * HEAD_DIM:(h + 1) * HEAD_DIM] = o.astype(o_ref.dtype)


def _attn_sample(c5, layer, page_table, zs, gates, kc, vc, win5, t_new, n_pages):
    bsz, tot_pages = page_table.shape
    n_groups = tot_pages // n_pages
    past_len = tot_pages * PAGE
    n_slc = -(-(past_len + t_new) // SLC_LEN)
    sel_lanes = -(-n_slc // LANES) * LANES
    n_win = win5.shape[2]
    tp = SUBLANES
    rows = GROUP * tp
    page_specs = [pl.BlockSpec((None, None, PAGE, 2 * N_KV, HEAD_DIM),
                               functools.partial(lambda b, g, pt, i: (layer, pt[b, g * n_pages + i], 0, 1, 0), i=i))
                  for i in range(n_pages)]

    def bmap(b, g, pt):
        return (b, 0, 0)

    grid_spec = pltpu.PrefetchScalarGridSpec(
        num_scalar_prefetch=1,
        grid=(bsz, n_groups),
        in_specs=page_specs + [
            pl.BlockSpec((None, tp, zs.shape[2]), bmap),
            pl.BlockSpec((None, kc.shape[1], KV_WIDTH), bmap),
            pl.BlockSpec((None, vc.shape[1], KV_WIDTH), bmap),
            pl.BlockSpec((None, None, n_win, 2 * N_KV, HEAD_DIM), lambda b, g, pt: (layer, b, 0, 0, 0)),
            pl.BlockSpec((None, tp, N_KV * LANES), bmap)],
        out_specs=pl.BlockSpec((None, tp, ATTN_WIDTH), bmap),
        scratch_shapes=[pltpu.VMEM((N_KV, rows, HEAD_DIM), F32),
                        pltpu.VMEM((N_KV, tp, sel_lanes), F32),
                        pltpu.VMEM((N_KV, rows, LANES), F32),
                        pltpu.VMEM((N_KV, rows, LANES), F32),
                        pltpu.VMEM((N_KV, rows, HEAD_DIM), F32)],
    )
    return pl.pallas_call(
        functools.partial(_attn_sample_kernel, n_pages=n_pages, n_groups=n_groups, t_new=t_new,
                          n_slc=n_slc, n_win=n_win),
        out_shape=jax.ShapeDtypeStruct((bsz, tp, ATTN_WIDTH), BF16),
        grid_spec=grid_spec,
        compiler_params=_cparams(("parallel", "arbitrary")),
        name="attn_sample",
    )(page_table, *([c5] * n_pages), zs, kc, vc, win5, gates)


CONV_HALO = 32
CONV_LANE_CHUNK = 256


def _conv_kernel(*refs, tc, multi_tile):
    if multi_tile:
        u_ref, up_ref, buf_ref, dw_ref, db_ref, lg_ref, lb_ref, o_ref, tail_ref, ext_sc, y_sc = refs
    else:
        u_ref, buf_ref, dw_ref, db_ref, lg_ref, lb_ref, o_ref, tail_ref, ext_sc, y_sc = refs
    t = pl.program_id(1)

    def glu(u):
        return u[:, :CONV_CH] * jax.nn.sigmoid(u[:, CONV_CH:])

    @pl.when(t == 0)
    def _():
        ext_sc[0:CONV_HALO, :] = buf_ref[...]

    if multi_tile:
        @pl.when(t > 0)
        def _():
            ext_sc[0:CONV_HALO, :] = glu(up_ref[...])

    ext_sc[CONV_HALO:CONV_HALO + tc, :] = glu(u_ref[...])
    off = CONV_HALO - (CONV_WIDTH - 1)
    n_ext = CONV_HALO + tc
    for c0 in range(0, CONV_CH, CONV_LANE_CHUNK):
        cs = slice(c0, c0 + CONV_LANE_CHUNK)
        ext = ext_sc[:, cs]
        shifted = [ext] + [pltpu.roll(ext, n_ext - s, 0) for s in range(1, SUBLANES)]
        acc = jnp.broadcast_to(db_ref[:, cs], (tc, CONV_LANE_CHUNK))
        for w in range(CONV_WIDTH):
            s = (off + w) % SUBLANES
            a = off + w - s
            acc = acc + shifted[s][a:a + tc] * dw_ref[w:w + 1, cs]
        y_sc[:, cs] = acc
    y = y_sc[...]
    mu = jnp.mean(y, axis=-1, keepdims=True)
    d = y - mu
    var = jnp.mean(d * d, axis=-1, keepdims=True)
    yn = d * lax.rsqrt(var + LN_EPS) * lg_ref[...] + lb_ref[...]
    o_ref[...] = (yn * jax.nn.sigmoid(yn)).astype(o_ref.dtype)

    @pl.when(t == pl.num_programs(1) - 1)
    def _():
        tail_ref[...] = ext_sc[tc:tc + CONV_HALO, :]


def _conv_group(z, buf, dw_w, dw_b, ln_g, ln_b, tc):
    bsz, t_len, _ = z.shape
    n_t = t_len // tc
    multi_tile = n_t > 1
    assert COL_U == 0 and (not multi_tile or tc % CONV_HALO == 0)
    halo_per_tile = max(tc // CONV_HALO, 1)

    def vec(v):
        return v.reshape(1, CONV_CH)

    cmap = lambda b, t: (0, 0)
    in_specs = [pl.BlockSpec((None, tc, 2 * CONV_CH), lambda b, t: (b, t, 0))]
    args = [z]
    if multi_tile:
        in_specs.append(pl.BlockSpec((None, CONV_HALO, 2 * CONV_CH),
                                     lambda b, t: (b, jnp.maximum(t * halo_per_tile - 1, 0), 0)))
        args.append(z)
    in_specs += [pl.BlockSpec((None, CONV_HALO, CONV_CH), lambda b, t: (b, 0, 0)),
                 pl.BlockSpec((CONV_WIDTH, CONV_CH), cmap),
                 pl.BlockSpec((1, CONV_CH), cmap), pl.BlockSpec((1, CONV_CH), cmap),
                 pl.BlockSpec((1, CONV_CH), cmap)]
    args += [buf, dw_w, vec(dw_b), vec(ln_g), vec(ln_b)]
    return pl.pallas_call(
        functools.partial(_conv_kernel, tc=tc, multi_tile=multi_tile),
        out_shape=(jax.ShapeDtypeStruct((bsz, t_len, CONV_CH), BF16),
                   jax.ShapeDtypeStruct((bsz, CONV_HALO, CONV_CH), F32)),
        grid=(bsz, n_t),
        in_specs=in_specs,
        out_specs=(pl.BlockSpec((None, tc, CONV_CH), lambda b, t: (b, t, 0)),
                   pl.BlockSpec((None, CONV_HALO, CONV_CH), lambda b, t: (b, 0, 0))),
        scratch_shapes=[pltpu.VMEM((CONV_HALO + tc, CONV_CH), F32), pltpu.VMEM((tc, CONV_CH), F32)],
        compiler_params=_cparams(("parallel", "arbitrary")),
        name="conv_group",
    )(*args)


def _rope_tables(pos):
    half = ROT_DIM // 2
    inv = ROPE_THETA ** (-jnp.arange(half, dtype=F32) / half)
    ang = pos.astype(F32)[:, None] * inv[None, :]
    cos, sin = jnp.cos(ang), jnp.sin(ang)
    n = pos.shape[0]
    zeros = jnp.zeros((n, half), F32)
    rest = HEAD_DIM - ROT_DIM
    c = jnp.concatenate([cos, cos, jnp.ones((n, rest), F32)], axis=1)
    s1 = jnp.concatenate([zeros, sin, jnp.zeros((n, rest), F32)], axis=1)
    s2 = jnp.concatenate([-sin, zeros, jnp.zeros((n, rest), F32)], axis=1)
    return c, s1, s2


def _prep_in_weights(w_in):
    depth, d, _ = w_in.shape
    kv_end = ATTN_WIDTH + KV_COLS
    gate_end = kv_end + 3 * N_HEADS
    w_main = jnp.concatenate([w_in[:, :, gate_end:], w_in[:, :, :kv_end]], axis=2).astype(BF16)
    wg = w_in[:, :, kv_end:gate_end].reshape(depth, d, 3, N_KV, GROUP).transpose(0, 1, 3, 2, 4)
    wg = jnp.pad(wg.reshape(depth, d, N_KV, 3 * GROUP), ((0, 0), (0, 0), (0, 0), (0, LANES - 3 * GROUP)))
    return w_main, wg.reshape(depth, d, N_KV * LANES).astype(BF16)


def _prep_cmp_weights(cmp_pe_l, cmp_w1_l, cmp_w2_l, cmp_b2_l):
    pef = jnp.pad(cmp_pe_l.reshape(2, 1, CMP_LEN * HEAD_DIM), ((0, 0), (0, SUBLANES - 1), (0, 0))).astype(BF16)
    w1 = cmp_w1_l.astype(BF16).reshape(2, CMP_LEN // CMP_STRIDE, CMP_STRIDE * HEAD_DIM, CMP_HIDDEN)
    return pef, w1, cmp_w2_l.astype(BF16), cmp_b2_l.reshape(2, 1, HEAD_DIM)


def _largest_tile(n, unit, cap):
    best = None
    for t in range(unit, min(n, cap) + 1, unit):
        if n % t == 0:
            best = t
    assert best is not None, (n, unit, cap)
    return best


def _dense_tail(x, attn, conv, layer, w_out_b, norm_ffn_l, w_gate_b, w_up_b, w_down_b, tm):
    d = x.shape[1]
    d_ff = w_gate_b.shape[2]
    h = _outproj(attn, conv, w_out_b, layer, x, tm, 512)
    hn = _rmsnorm(h, norm_ffn_l, BF16)
    act = _ffn_up(hn, w_gate_b, w_up_b, layer, tm, _largest_tile(d_ff, 2 * LANES, 512))
    return _ffn_down(act, w_down_b, layer, h, min(tm, 512), min(d, 1024), _largest_tile(d_ff, LANES, 6144))


def kernel(x_prompt, x_sample, cache_kv, cache_win, state_conv, page_table, norm_mix, w_in, cmp_pe, cmp_w1,
           cmp_w2, cmp_b2, conv_dw_w, conv_dw_b, conv_ln_g, conv_ln_b, w_out, norm_ffn, w_gate, w_up, w_down,
           norm_final):
    depth = w_in.shape[0]
    bsz, seq, d = x_prompt.shape
    dec_b, dec_t, _ = x_sample.shape
    n_pool, page_size = cache_kv.shape[1], cache_kv.shape[2]
    tot_pages = page_table.shape[1]
    past_len = tot_pages * page_size
    assert page_size == PAGE and seq % PAGE == 0 and dec_t <= SUBLANES
    assert conv_dw_w.shape[2] == CONV_CH and d == ATTN_WIDTH + CONV_CH
    pages_per_group = min(16, tot_pages, seq // PAGE)
    assert tot_pages % pages_per_group == 0 and (seq // PAGE) % pages_per_group == 0
    tp = SUBLANES
    ms = dec_b * tp

    tm_p = min(512, seq)
    tabs_p = _rope_tables(jnp.arange(seq, dtype=I32))
    tabs_s = tuple(jnp.tile(t, (dec_b, 1)) for t in _rope_tables(past_len + jnp.arange(tp, dtype=I32)))

    hp = x_prompt.reshape(bsz * seq, d)
    hs = jnp.pad(x_sample, ((0, 0), (0, tp - dec_t), (0, 0))).reshape(ms, d)
    cache5 = cache_kv.reshape(depth, n_pool, page_size, 4 * N_KV, HEAD_DIM)
    win5 = cache_win.reshape(depth, dec_b, cache_win.shape[2], 2 * N_KV, HEAD_DIM)
    zero_buf = jnp.zeros((bsz, CONV_HALO, CONV_CH), F32)
    tm_dense = min(1024, bsz * seq)

    w_main, wg = _prep_in_weights(w_in)
    w_out_b, w_gate_b, w_up_b, w_down_b = (w.astype(BF16) for w in (w_out, w_gate, w_up, w_down))

    z_layers, convp, kvs, wins, convs = [], [], [], [], []
    for l in range(depth):
        cw = _prep_cmp_weights(cmp_pe[l], cmp_w1[l], cmp_w2[l], cmp_b2[l])

        z, gates, zc = _inproj(hp, norm_mix[l], w_main, wg, l, tabs_p, seq // tm_p, tm_p)
        z_layers.append(z)
        z3 = z.reshape(bsz, seq, z.shape[1])
        kc, vc = _compress_prompt(zc.reshape(bsz, seq, 2 * N_KV, HEAD_DIM), cw, pages_per_group)
        attn = _attn_prompt(z3, gates.reshape(bsz, seq, N_KV * LANES), kc, vc)
        conv, tail = _conv_group(z3, zero_buf, conv_dw_w[l], conv_dw_b[l], conv_ln_g[l], conv_ln_b[l], PAGE)
        hp = _dense_tail(hp, attn.reshape(bsz * seq, ATTN_WIDTH), conv.reshape(bsz * seq, CONV_CH), l, w_out_b,
                         norm_ffn[l], w_gate_b, w_up_b, w_down_b, tm_dense)
        convp.append(tail[:, CONV_HALO - (CONV_WIDTH - 1):])

        zs, gates_s, _ = _inproj(hs, norm_mix[l], w_main, wg, l, tabs_s, 1, ms)
        zs3 = zs.reshape(dec_b, tp, zs.shape[1])
        kc_s, vc_s = _compress_sample(cache5, l, page_table, cw, pages_per_group)
        attn_s = _attn_sample(cache5, l, page_table, zs3, gates_s.reshape(dec_b, tp, N_KV * LANES),
                              kc_s, vc_s, win5, dec_t, pages_per_group)
        buf_s = jnp.pad(state_conv[l], ((0, 0), (CONV_HALO - (CONV_WIDTH - 1), 0), (0, 0)))
        conv_s, tail_s = _conv_group(zs3, buf_s, conv_dw_w[l], conv_dw_b[l], conv_ln_g[l], conv_ln_b[l], tp)
        hs = _dense_tail(hs, attn_s.reshape(ms, ATTN_WIDTH), conv_s.reshape(ms, CONV_CH), l, w_out_b, norm_ffn[l],
                         w_gate_b, w_up_b, w_down_b, ms)
        kvs.append(zs3[:, :dec_t, COL_KV:COL_KV + 4 * KV_WIDTH].reshape(dec_b, dec_t, 4, N_KV, HEAD_DIM))
        new_win_s = zs3[:, :dec_t, COL_KV + 4 * KV_WIDTH:COL_KV + KV_COLS].reshape(dec_b, dec_t, 2, N_KV, HEAD_DIM)
        win_all = jnp.concatenate([cache_win[l], new_win_s], axis=1)
        wins.append(win_all[:, win_all.shape[1] - min(WINDOW, win_all.shape[1]):])
        new_glu = tail_s[:, CONV_HALO - tp:CONV_HALO - tp + dec_t]
        ext_s = jnp.concatenate([state_conv[l], new_glu], axis=1)
        convs.append(ext_s[:, ext_s.shape[1] - (CONV_WIDTH - 1):])

    y_prompt = _rmsnorm(hp, norm_final, F32).reshape(bsz, seq, d)
    y_sample = _rmsnorm(hs, norm_final, F32).reshape(dec_b, tp, d)[:, :dec_t]
    keep = min(WINDOW, seq)
    tiles_per_batch = seq // keep
    assert seq % keep == 0
    kv_rows = _rows_out(z_layers, bsz * tiles_per_batch, keep, COL_KV, 4 * N_KV, lambda i: i)
    win_rows = _rows_out(z_layers, bsz, keep, COL_KV + 4 * KV_WIDTH, 2 * N_KV,
                         lambda i: i * tiles_per_batch + tiles_per_batch - 1)
    kv_prompt = kv_rows.reshape(depth, bsz, seq, 4, N_KV, HEAD_DIM)
    win_prompt = win_rows.reshape(depth, bsz, keep, 2, N_KV, HEAD_DIM)
    return (y_prompt, y_sample, kv_prompt, win_prompt, jnp.stack(convp),
            jnp.stack(kvs), jnp.stack(wins), jnp.stack(convs))
```

```python
import functools
import math

import jax
import jax.numpy as jnp
from jax import lax
from jax.experimental import pallas as pl
from jax.experimental.pallas import tpu as pltpu

F32 = jnp.float32
BF16 = jnp.bfloat16
I32 = jnp.int32

N_HEADS = 16
N_KV = 4
HEAD_DIM = 128
GROUP = N_HEADS // N_KV
ATTN_WIDTH = N_HEADS * HEAD_DIM
KV_WIDTH = N_KV * HEAD_DIM
CONV_CH = 2048
CONV_WIDTH = 31
ROT_DIM = HEAD_DIM // 4
ROPE_THETA = 500000.0
CMP_LEN = 32
CMP_STRIDE = 16
CMP_HIDDEN = 2 * HEAD_DIM
SLC_LEN = 64
SLC_SHIFT = 6
N_SEL = 16
WINDOW = 512
SCALE = HEAD_DIM ** -0.5
LOG2E = 1.0 / math.log(2.0)
SLC_KEY_BLOCK = 512
NEG = -1e30
FORCE = 1e9
RMS_EPS = 1e-6
LN_EPS = 1e-5

LANES = 128
SUBLANES = 8
VMEM_LIMIT = 56 * 1024 * 1024

PAGE = 128
CHUNKS_PER_PAGE = PAGE // CMP_STRIDE
COL_U = 0
COL_Q = 2 * CONV_CH
COL_KV = COL_Q + ATTN_WIDTH
KV_COLS = 6 * KV_WIDTH
PROJ_TN = 512


def _cparams(sem):
    return pltpu.CompilerParams(dimension_semantics=sem, vmem_limit_bytes=VMEM_LIMIT)


def _dot(a, b):
    return jnp.dot(a, b, preferred_element_type=F32)


def _dot_nt(a, b):
    return lax.dot_general(a, b, (((1,), (1,)), ((), ())), preferred_element_type=F32)


def _split3_dot(x, m_bf16):
    hi = x.astype(BF16)
    r1 = x - hi.astype(F32)
    mid = r1.astype(BF16)
    lo = (r1 - mid.astype(F32)).astype(BF16)
    return _dot(hi, m_bf16) + _dot(mid, m_bf16) + _dot(lo, m_bf16)


def _rmsnorm_kernel(x_ref, g_ref, o_ref):
    x = x_ref[...]
    ms = jnp.mean(x * x, axis=-1, keepdims=True)
    o_ref[...] = (x * lax.rsqrt(ms + RMS_EPS) * g_ref[...]).astype(o_ref.dtype)


def _rmsnorm(x, g, out_dtype):
    m, d = x.shape
    tm = min(m, 256)
    return pl.pallas_call(
        _rmsnorm_kernel,
        out_shape=jax.ShapeDtypeStruct((m, d), out_dtype),
        grid=(m // tm,),
        in_specs=[pl.BlockSpec((tm, d), lambda i: (i, 0)),
                  pl.BlockSpec((1, d), lambda i: (0, 0))],
        out_specs=pl.BlockSpec((tm, d), lambda i: (i, 0)),
        compiler_params=_cparams(("parallel",)),
        name="rmsnorm",
    )(x, g.reshape(1, d))


def _inproj_kernel(a_ref, w_ref, c_ref, s1_ref, s2_ref, z_ref, zc_ref):
    j = pl.program_id(1)
    n_heads_tile = PROJ_TN // HEAD_DIM
    acc = _dot(a_ref[...], w_ref[...])
    q0 = COL_Q // PROJ_TN
    k0 = COL_KV // PROJ_TN
    is_rope = ((j >= q0) & (j <= k0)) | (j == k0 + 2) | (j == k0 + 4)

    @pl.when(is_rope)
    def _():
        c = c_ref[...]
        s1 = s1_ref[...]
        s2 = s2_ref[...]
        rotated = []
        for h in range(n_heads_tile):
            x = acc[:, h * HEAD_DIM:(h + 1) * HEAD_DIM]
            rotated.append(x * c + pltpu.roll(x, ROT_DIM // 2, 1) * s1
                           + pltpu.roll(x, HEAD_DIM - ROT_DIM // 2, 1) * s2)
            z_ref[:, h * HEAD_DIM:(h + 1) * HEAD_DIM] = rotated[h]

        @pl.when(j == k0)
        def _():
            for h in range(n_heads_tile):
                zc_ref[:, h, :] = rotated[h]

    @pl.when(jnp.logical_not(is_rope))
    def _():
        z_ref[...] = acc

        @pl.when(j == k0 + 1)
        def _():
            for h in range(n_heads_tile):
                zc_ref[:, n_heads_tile + h, :] = acc[:, h * HEAD_DIM:(h + 1) * HEAD_DIM]


def _inproj(xn, w, layer, tabs, tab_tiles, tm):
    m, k = xn.shape
    n = w.shape[2]
    c, s1, s2 = tabs
    tab_spec = pl.BlockSpec((tm, HEAD_DIM), lambda i, j: (i % tab_tiles, 0))
    return pl.pallas_call(
        _inproj_kernel,
        out_shape=(jax.ShapeDtypeStruct((m, n), F32),
                   jax.ShapeDtypeStruct((m, 2 * N_KV, HEAD_DIM), F32)),
        grid=(m // tm, n // PROJ_TN),
        in_specs=[pl.BlockSpec((tm, k), lambda i, j: (i, 0)),
                  pl.BlockSpec((None, k, PROJ_TN), lambda i, j: (layer, 0, j)),
                  tab_spec, tab_spec, tab_spec],
        out_specs=(pl.BlockSpec((tm, PROJ_TN), lambda i, j: (i, j)),
                   pl.BlockSpec((tm, 2 * N_KV, HEAD_DIM), lambda i, j: (i, 0, 0))),
        compiler_params=_cparams(("parallel", "arbitrary")),
        name="inproj",
    )(xn, w, c, s1, s2)


def _gates_kernel(a_ref, wg_ref, g_ref):
    g_ref[...] = jax.nn.sigmoid(_dot(a_ref[...], wg_ref[...]))


def _gates(xn, wg, layer, tm):
    m, k = xn.shape
    n = wg.shape[2]
    return pl.pallas_call(
        _gates_kernel,
        out_shape=jax.ShapeDtypeStruct((m, n), F32),
        grid=(m // tm,),
        in_specs=[pl.BlockSpec((tm, k), lambda i: (i, 0)),
                  pl.BlockSpec((None, k, n), lambda i: (layer, 0, 0))],
        out_specs=pl.BlockSpec((tm, n), lambda i: (i, 0)),
        compiler_params=_cparams(("parallel",)),
        name="gates",
    )(xn, wg)


def _rows_out_kernel(*refs, depth):
    z_refs = refs[:depth]
    o_ref = refs[depth]
    layer = pl.program_id(0)
    for li, z_ref in enumerate(z_refs):
        @pl.when(layer == li)
        def _(z_ref=z_ref):
            for r in range(o_ref.shape[1]):
                o_ref[:, r, :] = z_ref[:, r * HEAD_DIM:(r + 1) * HEAD_DIM]


def _rows_out(zs, n_tiles, tm, col0, n_head_rows, row_tile_of):
    depth = len(zs)
    width = n_head_rows * HEAD_DIM
    assert col0 % width == 0
    in_specs = [pl.BlockSpec((tm, width),
                             functools.partial(lambda l, i, li: (jnp.where(l == li, row_tile_of(i), 0), col0 // width),
                                               li=li))
                for li in range(depth)]
    return pl.pallas_call(
        functools.partial(_rows_out_kernel, depth=depth),
        out_shape=jax.ShapeDtypeStruct((depth, n_tiles * tm, n_head_rows, HEAD_DIM), F32),
        grid=(depth, n_tiles),
        in_specs=in_specs,
        out_specs=pl.BlockSpec((None, tm, n_head_rows, HEAD_DIM), lambda l, i: (l, i, 0, 0)),
        compiler_params=_cparams(("arbitrary", "arbitrary")),
        name="rows_out",
    )(*zs)


def _outproj_kernel(a1_ref, a2_ref, w1_ref, w2_ref, r_ref, o_ref):
    o_ref[...] = r_ref[...] + _dot(a1_ref[...], w1_ref[...]) + _dot(a2_ref[...], w2_ref[...])


def _outproj(a1, a2, w, layer, res, tm, tn):
    m, k1 = a1.shape
    k2 = a2.shape[1]
    n = w.shape[2]
    assert k1 == k2 and w.shape[1] == k1 + k2
    return pl.pallas_call(
        _outproj_kernel,
        out_shape=jax.ShapeDtypeStruct((m, n), F32),
        grid=(m // tm, n // tn),
        in_specs=[pl.BlockSpec((tm, k1), lambda i, j: (i, 0)),
                  pl.BlockSpec((tm, k2), lambda i, j: (i, 0)),
                  pl.BlockSpec((None, k1, tn), lambda i, j: (layer, 0, j)),
                  pl.BlockSpec((None, k2, tn), lambda i, j: (layer, 1, j)),
                  pl.BlockSpec((tm, tn), lambda i, j: (i, j))],
        out_specs=pl.BlockSpec((tm, tn), lambda i, j: (i, j)),
        compiler_params=_cparams(("parallel", "arbitrary")),
        name="outproj",
    )(a1, a2, w, w, res)


def _ffn_up_kernel(a_ref, wg_ref, wu_ref, o_ref):
    a = a_ref[...]
    gate = _dot(a, wg_ref[...])
    up = _dot(a, wu_ref[...])
    o_ref[...] = (gate * jax.nn.sigmoid(gate) * up).astype(o_ref.dtype)


def _ffn_up(a, wg, wu, layer, tm, tn):
    m, k = a.shape
    n = wg.shape[2]
    return pl.pallas_call(
        _ffn_up_kernel,
        out_shape=jax.ShapeDtypeStruct((m, n), BF16),
        grid=(m // tm, n // tn),
        in_specs=[pl.BlockSpec((tm, k), lambda i, j: (i, 0)),
                  pl.BlockSpec((None, k, tn), lambda i, j: (layer, 0, j)),
                  pl.BlockSpec((None, k, tn), lambda i, j: (layer, 0, j))],
        out_specs=pl.BlockSpec((tm, tn), lambda i, j: (i, j)),
        compiler_params=_cparams(("parallel", "arbitrary")),
        name="ffn_up",
    )(a, wg, wu)


def _ffn_down_kernel(a_ref, w_ref, r_ref, o_ref, acc_ref):
    kk = pl.program_id(2)

    @pl.when(kk == 0)
    def _():
        acc_ref[...] = r_ref[...]

    acc_ref[...] += _dot(a_ref[...], w_ref[...])

    @pl.when(kk == pl.num_programs(2) - 1)
    def _():
        o_ref[...] = acc_ref[...]


def _ffn_down(a, w, layer, res, tm, tn, tk):
    m, k = a.shape
    n = w.shape[2]
    return pl.pallas_call(
        _ffn_down_kernel,
        out_shape=jax.ShapeDtypeStruct((m, n), F32),
        grid=(m // tm, n // tn, k // tk),
        in_specs=[pl.BlockSpec((tm, tk), lambda i, j, kk: (i, kk)),
                  pl.BlockSpec((None, tk, tn), lambda i, j, kk: (layer, kk, j)),
                  pl.BlockSpec((tm, tn), lambda i, j, kk: (i, j))],
        out_specs=pl.BlockSpec((tm, tn), lambda i, j, kk: (i, j)),
        scratch_shapes=[pltpu.VMEM((tm, tn), F32)],
        compiler_params=_cparams(("parallel", "parallel", "arbitrary")),
        name="ffn_down",
    )(a, w, res)


def _gelu_tanh(x):
    return 0.5 * x * (1.0 + jnp.tanh(math.sqrt(2.0 / math.pi) * (x + 0.044715 * (x * x * x))))


def _compress_kernel(*refs, n_pages, n_prefetch):
    page_refs = refs[n_prefetch:n_prefetch + n_pages]
    pef_ref, w1_ref, w2_ref, b2_ref, kc_ref, vc_ref, carry_ref = refs[n_prefetch + n_pages:]
    grp = pl.program_id(1)
    n_chunks = n_pages * CHUNKS_PER_PAGE
    half_k = CMP_STRIDE * HEAD_DIM

    @pl.when(grp == 0)
    def _():
        carry_ref[...] = jnp.zeros_like(carry_ref)

    rows = N_KV * n_chunks
    pairs = CHUNKS_PER_PAGE // 2
    lo = lax.broadcasted_iota(I32, (pairs, CMP_STRIDE, 2 * N_KV, HEAD_DIM), 2) < N_KV
    xk, xv = [], []
    for p in page_refs:
        x = p[...].reshape(pairs, 2, CMP_STRIDE, 2 * N_KV, HEAD_DIM)
        even, odd = x[:, 0], x[:, 1]
        kp = jnp.where(lo, even, pltpu.roll(odd, N_KV, 2))
        vp = jnp.where(lo, pltpu.roll(even, N_KV, 2), odd)
        for t, dst in ((kp, xk), (vp, xv)):
            dst.append(jnp.concatenate([t[:, j] for j in range(CMP_STRIDE)], axis=-1)
                       .reshape(pairs * 2 * N_KV, CMP_STRIDE * HEAD_DIM))
    row = lax.broadcasted_iota(I32, (rows, CMP_HIDDEN), 0)
    for kind, parts, out_ref in ((0, xk, kc_ref), (1, xv, vc_ref)):
        x_all = jnp.concatenate(parts, axis=0).astype(BF16)
        a0 = _dot(x_all, w1_ref[kind, 0])
        a1 = _dot(x_all, w1_ref[kind, 1])
        pe = pef_ref[kind]
        pe_term = (_dot(pe[:, :half_k], w1_ref[kind, 0]) + _dot(pe[:, half_k:], w1_ref[kind, 1]))[0:1, :]
        shifted = pltpu.roll(jnp.where(row >= rows - N_KV, carry_ref[kind], a0), N_KV, 0)
        pre = shifted + a1 + pe_term
        out_ref[...] = (_dot(_gelu_tanh(pre).astype(BF16), w2_ref[kind]) + b2_ref[kind]).astype(out_ref.dtype)
        carry_ref[kind] = a0


def _compress_prompt(zc, cw, n_pages):
    bsz, t_len = zc.shape[:2]
    specs = [pl.BlockSpec((None, PAGE, 2 * N_KV, HEAD_DIM),
                          functools.partial(lambda b, g, i: (b, g * n_pages + i, 0, 0), i=i))
             for i in range(n_pages)]
    return _compress(zc, specs, bsz, (t_len // PAGE) // n_pages, n_pages, cw)


def _compress_sample(c5, layer, page_table, cw, n_pages):
    bsz, tot_pages = page_table.shape
    specs = [pl.BlockSpec((None, None, PAGE, 2 * N_KV, HEAD_DIM),
                          functools.partial(lambda b, g, pt, i: (layer, pt[b, g * n_pages + i], 0, 0, 0), i=i))
             for i in range(n_pages)]
    return _compress(c5, specs, bsz, tot_pages // n_pages, n_pages, cw, prefetch=(page_table,))


def _compress(src, page_specs, n_batch, n_groups, n_pages, cw, prefetch=()):
    pef, w1, w2, b2 = cw
    n_chunks = n_pages * CHUNKS_PER_PAGE

    def const_spec(x):
        return pl.BlockSpec(x.shape, lambda b, g, *pt: (0,) * x.ndim)

    out_spec = pl.BlockSpec((None, N_KV * n_chunks, HEAD_DIM), lambda b, g, *pt: (b, g, 0))
    out_shape = jax.ShapeDtypeStruct((n_batch, n_groups * N_KV * n_chunks, HEAD_DIM), BF16)
    grid_spec = pltpu.PrefetchScalarGridSpec(
        num_scalar_prefetch=len(prefetch),
        grid=(n_batch, n_groups),
        in_specs=list(page_specs) + [const_spec(pef), const_spec(w1), const_spec(w2), const_spec(b2)],
        out_specs=(out_spec, out_spec),
        scratch_shapes=[pltpu.VMEM((2, N_KV * n_chunks, CMP_HIDDEN), F32)],
    )
    kc, vc = pl.pallas_call(
        functools.partial(_compress_kernel, n_pages=n_pages, n_prefetch=len(prefetch)),
        out_shape=(out_shape, out_shape),
        grid_spec=grid_spec,
        compiler_params=_cparams(("parallel", "arbitrary")),
        name="compress",
    )(*prefetch, *([src] * n_pages), pef, w1, w2, b2)
    return (kc.reshape(n_batch, n_groups * n_chunks, KV_WIDTH), vc.reshape(n_batch, n_groups * n_chunks, KV_WIDTH))


def _flash_init(m_sc, l_sc, acc_sc, br):
    m_sc[br] = jnp.full(m_sc.shape[1:], -jnp.inf, F32)
    l_sc[br] = jnp.zeros(l_sc.shape[1:], F32)
    acc_sc[br] = jnp.zeros(acc_sc.shape[1:], F32)


def _flash_update(m_sc, l_sc, acc_sc, br, r0, nr, s, v):
    m_old = m_sc[br, r0:r0 + nr, :]
    m_new = jnp.maximum(m_old, jnp.max(s, axis=1, keepdims=True))
    alpha = jnp.exp(m_old - m_new)
    p = jnp.exp(s - (m_new if s.shape[1] == LANES else m_new[:, 0:1]))
    l_sc[br, r0:r0 + nr, :] = alpha * l_sc[br, r0:r0 + nr, :] + jnp.sum(p, axis=1, keepdims=True)
    acc_sc[br, r0:r0 + nr, :] = alpha * acc_sc[br, r0:r0 + nr, :] + _dot(p.astype(BF16), v)
    m_sc[br, r0:r0 + nr, :] = m_new


def _overlap_matrix(shape, n_slc):
    m = lax.broadcasted_iota(I32, shape, 0)
    j = lax.broadcasted_iota(I32, shape, 1)
    start = (m - 1) * CMP_STRIDE
    ov = (m >= 1) & (start < j * SLC_LEN + SLC_LEN) & (start + CMP_LEN > j * SLC_LEN) & (j < n_slc)
    return jnp.where(ov, 1.0, 0.0).astype(BF16)


def _lane_tile_reduce(x, op):
    out = x[..., 0:LANES]
    for i in range(1, x.shape[-1] // LANES):
        out = op(out, x[..., i * LANES:(i + 1) * LANES])
    return out


def _attn_prompt_kernel(q_ref, kc_ref, vc_ref, ks_ref, vs_ref, kw_ref, vw_ref, g_ref, o_ref,
                        ksb_sc, vsb_sc, kwb_sc, vwb_sc, s_sc, m_sc, l_sc, acc_sc, *, n_slc, kb, win_keys):
    qt = pl.program_id(2)
    tq = PAGE
    rows = GROUP * tq
    t_len = ks_ref.shape[0]

    @pl.when(qt == 0)
    def _():
        ksb_sc[...] = ks_ref[...].astype(BF16)
        vsb_sc[...] = vs_ref[...].astype(BF16)
        kwb_sc[...] = kw_ref[...].astype(BF16)
        vwb_sc[...] = vw_ref[...].astype(BF16)

    q = q_ref[...] * (SCALE * LOG2E)
    q_all = jnp.concatenate([q[:, g * HEAD_DIM:(g + 1) * HEAD_DIM] for g in range(GROUP)],
                            axis=0).astype(BF16)
    row = lax.broadcasted_iota(I32, (tq, LANES), 0)
    lane = lax.broadcasted_iota(I32, (tq, LANES), 1)
    pos = qt * tq + row

    valid = (lane >= 1) & (lane * CMP_STRIDE + (CMP_LEN - CMP_STRIDE - 1) <= pos)
    vbias = jnp.where(valid, 0.0, NEG)
    vf = jnp.where(valid, 1.0, 0.0)
    s3 = _dot_nt(q_all, kc_ref[...]).reshape(GROUP, tq, LANES) + vbias[None]
    e = jnp.exp2(s3 - jnp.max(s3, axis=2, keepdims=True)) * vf[None]
    denom = jnp.sum(e, axis=2, keepdims=True)
    p3 = e * (1.0 / jnp.maximum(denom, 1e-30))
    o_cmp = _dot(p3.reshape(rows, LANES).astype(BF16), vc_ref[...])
    psum = p3[0] + p3[1] + p3[2] + p3[3]

    imp = _split3_dot(psum, _overlap_matrix((LANES, LANES), n_slc))
    cur = pos >> SLC_SHIFT
    forced = (lane == 0) | (lane == cur) | (lane == cur - 1)
    score = jnp.where(lane <= cur, jnp.where(forced, FORCE, imp), -1.0)
    score = jnp.where(lane < n_slc, score, -2.0)
    n_rank = -(-n_slc // SUBLANES) * SUBLANES
    score_t = score.T[0:n_rank]
    jrow = lax.broadcasted_iota(I32, (n_rank, tq), 0)
    rank_t = jnp.zeros((n_rank, tq), F32)
    for jp in range(n_slc):
        other = score_t[jp:jp + 1, :]
        tie = jnp.where(jrow > jp, 1.0, 0.0)
        rank_t = rank_t + jnp.where(other > score_t, 1.0, jnp.where(other == score_t, tie, 0.0))
    sel_t = jnp.where((rank_t < min(N_SEL, n_slc)) & (jrow < n_slc), 1.0, 0.0)
    sel = jnp.concatenate([sel_t, jnp.zeros((LANES - n_rank, tq), F32)], axis=0).T.astype(BF16)

    m_sc[...] = jnp.full(m_sc.shape, -jnp.inf, F32)
    l_sc[...] = jnp.zeros(l_sc.shape, F32)
    acc_sc[...] = jnp.zeros(acc_sc.shape, F32)
    n_kb = t_len // kb
    brow = lax.broadcasted_iota(I32, (LANES, kb), 0)
    blane = lax.broadcasted_iota(I32, (tq, kb), 1)
    bpos = qt * tq + lax.broadcasted_iota(I32, (tq, kb), 0)
    for cb in range(n_kb):
        @pl.when(cb * kb <= qt * tq)
        def _(cb=cb):
            key = cb * kb + blane
            expand = jnp.where(((cb * kb + lax.broadcasted_iota(I32, (LANES, kb), 1)) >> SLC_SHIFT) == brow,
                               1.0, 0.0).astype(BF16)
            selm = _dot(sel, expand)
            bias = jnp.where((selm > 0.5) & (key <= bpos), 0.0, NEG)
            s = _dot_nt(q_all, ksb_sc[cb * kb:(cb + 1) * kb, :]).reshape(GROUP, tq, kb) + bias[None]
            s_sc[:, cb * kb:(cb + 1) * kb] = s.reshape(rows, kb)
            m_sc[...] = jnp.maximum(m_sc[...], _lane_tile_reduce(s, jnp.maximum).reshape(rows, LANES))

    m_row = jnp.max(m_sc[...], axis=1, keepdims=True)
    for cb in range(n_kb):
        @pl.when(cb * kb <= qt * tq)
        def _(cb=cb):
            p = jnp.exp2(s_sc[:, cb * kb:(cb + 1) * kb] - m_row)
            l_sc[...] += _lane_tile_reduce(p, jnp.add)
            acc_sc[...] += _dot(p.astype(BF16), vsb_sc[cb * kb:(cb + 1) * kb, :])

    o_slc = acc_sc[...] * (1.0 / jnp.sum(l_sc[...], axis=1, keepdims=True))

    start = pl.multiple_of(jnp.maximum(qt * tq + tq - win_keys, 0), PAGE)
    wlane = lax.broadcasted_iota(I32, (tq, win_keys), 1)
    wdiff = (qt * tq + lax.broadcasted_iota(I32, (tq, win_keys), 0)) - (start + wlane)
    wbias = jnp.where((wdiff >= 0) & (wdiff < WINDOW), 0.0, NEG)
    sw = _dot_nt(q_all, kwb_sc[pl.ds(start, win_keys), :]).reshape(GROUP, tq, win_keys) + wbias[None]
    pw = jnp.exp2(sw - jnp.max(sw, axis=2, keepdims=True))
    lw = jnp.sum(pw, axis=2, keepdims=True).reshape(rows, 1)
    o_win = _dot(pw.reshape(rows, win_keys).astype(BF16), vwb_sc[pl.ds(start, win_keys), :]) * (1.0 / lw)

    gates = g_ref[...]
    for g in range(GROUP):
        r = slice(g * tq, (g + 1) * tq)
        o = (gates[:, g:g + 1] * o_cmp[r] + gates[:, GROUP + g:GROUP + g + 1] * o_slc[r]
             + gates[:, 2 * GROUP + g:2 * GROUP + g + 1] * o_win[r])
        o_ref[:, g * HEAD_DIM:(g + 1) * HEAD_DIM] = o.astype(o_ref.dtype)


def _attn_prompt(z, gates, kc, vc):
    bsz, t_len, _ = z.shape
    n_slc = -(-t_len // SLC_LEN)
    assert t_len % PAGE == 0 and n_slc <= LANES and kc.shape[1] <= LANES
    n_cmp_rows = kc.shape[1]
    assert n_cmp_rows == LANES, "compressed keys are laid out on one 128-lane tile"
    kv0 = COL_KV // HEAD_DIM

    def kv_spec(kind):
        return pl.BlockSpec((None, t_len, HEAD_DIM), lambda b, k, t: (b, 0, kv0 + kind * N_KV + k))

    cmp_spec = pl.BlockSpec((None, n_cmp_rows, HEAD_DIM), lambda b, k, t: (b, 0, k))
    q_blk0 = COL_Q // (GROUP * HEAD_DIM)
    kb = min(SLC_KEY_BLOCK, t_len)
    win_keys = min(WINDOW + PAGE, t_len)
    assert t_len % kb == 0
    rows = GROUP * PAGE
    kv_scratch = pltpu.VMEM((t_len, HEAD_DIM), BF16)
    return pl.pallas_call(
        functools.partial(_attn_prompt_kernel, n_slc=n_slc, kb=kb, win_keys=win_keys),
        out_shape=jax.ShapeDtypeStruct((bsz, t_len, ATTN_WIDTH), BF16),
        grid=(bsz, N_KV, t_len // PAGE),
        in_specs=[pl.BlockSpec((None, PAGE, GROUP * HEAD_DIM), lambda b, k, t: (b, t, q_blk0 + k)),
                  cmp_spec, cmp_spec,
                  kv_spec(2), kv_spec(3), kv_spec(4), kv_spec(5),
                  pl.BlockSpec((None, PAGE, LANES), lambda b, k, t: (b, t, k))],
        out_specs=pl.BlockSpec((None, PAGE, GROUP * HEAD_DIM), lambda b, k, t: (b, t, k)),
        scratch_shapes=[kv_scratch, kv_scratch, kv_scratch, kv_scratch,
                        pltpu.VMEM((rows, t_len), F32),
                        pltpu.VMEM((rows, LANES), F32),
                        pltpu.VMEM((rows, LANES), F32),
                        pltpu.VMEM((rows, HEAD_DIM), F32)],
        compiler_params=_cparams(("parallel", "parallel", "arbitrary")),
        name="attn_prompt",
    )(z, kc, vc, z, z, z, z, gates)


def _attn_sample_kernel(*refs, n_pages, n_groups, t_new, n_slc, n_win):
    page_refs = refs[1:1 + n_pages]
    zs_ref, kc_ref, vc_ref, win_ref, g_ref, o_ref, ocmp_sc, sel_sc, m_sc, l_sc, acc_sc = refs[1 + n_pages:]
    pages = [p.reshape(PAGE * 2 * N_KV, HEAD_DIM) for p in page_refs]
    win_rows = win_ref.reshape(n_win * 2 * N_KV, HEAD_DIM)
    grp = pl.program_id(1)
    tp = SUBLANES
    rows = GROUP * tp
    n_cmp_rows = kc_ref.shape[0]
    sel_lanes = sel_sc.shape[2]
    past_len = n_groups * n_pages * PAGE

    def q_heads(k):
        c0 = COL_Q + k * GROUP * HEAD_DIM
        q = zs_ref[:, c0:c0 + GROUP * HEAD_DIM] * SCALE
        return jnp.concatenate([q[:, g * HEAD_DIM:(g + 1) * HEAD_DIM] for g in range(GROUP)],
                               axis=0).astype(BF16)

    @pl.when(grp == 0)
    def _():
        ovb = _overlap_matrix((n_cmp_rows, sel_lanes), n_slc)
        mlane = lax.broadcasted_iota(I32, (rows, n_cmp_rows), 1)
        vbias = jnp.where(mlane >= 1, 0.0, NEG)
        lane = lax.broadcasted_iota(I32, (tp, sel_lanes), 1)
        lane_f = lane.astype(F32)
        tok = lax.broadcasted_iota(I32, (tp, sel_lanes), 0)
        cur = (past_len + tok) >> SLC_SHIFT
        forced = (lane == 0) | (lane == cur) | (lane == cur - 1)
        for k in range(N_KV):
            qk = q_heads(k)
            kc = kc_ref[:, k * HEAD_DIM:(k + 1) * HEAD_DIM]
            vc = vc_ref[:, k * HEAD_DIM:(k + 1) * HEAD_DIM]
            s = _dot_nt(qk, kc) + vbias
            e = jnp.exp(s - jnp.max(s, axis=1, keepdims=True))
            p = e * (1.0 / jnp.sum(e, axis=1, keepdims=True))
            ocmp_sc[k] = _dot(p.astype(BF16), vc)
            psum = p[0:tp]
            for g in range(1, GROUP):
                psum = psum + p[g * tp:(g + 1) * tp]
            imp = _split3_dot(psum, ovb)
            score = jnp.where(lane <= cur, jnp.where(forced, FORCE, imp), -1.0)
            score = jnp.where(lane < n_slc, score, -3e38)
            sel = jnp.zeros((tp, sel_lanes), F32)
            for _ in range(min(N_SEL, n_slc)):
                top = jnp.max(score, axis=1, keepdims=True)
                idx = jnp.min(jnp.where(score == top, lane_f, float(sel_lanes)), axis=1, keepdims=True)
                hit = lane_f == idx
                sel = jnp.where(hit, 1.0, sel)
                score = jnp.where(hit, -3e38, score)
            sel_sc[k] = sel
            _flash_init(m_sc, l_sc, acc_sc, k)

    nk = n_pages * PAGE
    jrow = lax.broadcasted_iota(I32, (sel_lanes, nk), 0)
    klane = lax.broadcasted_iota(I32, (sel_lanes, nk), 1)
    expand = jnp.where(jrow == grp * (nk // SLC_LEN) + (klane >> SLC_SHIFT), 1.0, 0.0).astype(BF16)
    for k in range(N_KV):
        kk = jnp.concatenate([p[pl.ds(k, PAGE, stride=2 * N_KV), :].astype(BF16) for p in pages], axis=0)
        vv = jnp.concatenate([p[pl.ds(N_KV + k, PAGE, stride=2 * N_KV), :].astype(BF16) for p in pages], axis=0)
        selm = _dot(sel_sc[k].astype(BF16), expand)
        bias = jnp.where(selm > 0.5, 0.0, NEG)
        bias = jnp.concatenate([bias] * GROUP, axis=0)
        _flash_update(m_sc, l_sc, acc_sc, k, 0, rows, _dot_nt(q_heads(k), kk) + bias, vv)

    @pl.when(grp == n_groups - 1)
    def _():
        tr = lax.broadcasted_iota(I32, (rows, tp), 0) % tp
        tc = lax.broadcasted_iota(I32, (rows, tp), 1)
        new_bias = jnp.where((tc <= tr) & (tc < t_new), 0.0, NEG)
        wr = lax.broadcasted_iota(I32, (rows, n_win), 0) % tp
        wc = lax.broadcasted_iota(I32, (rows, n_win), 1)
        wdiff = wr + n_win - wc
        win_bias = jnp.where((wdiff >= 0) & (wdiff < WINDOW), 0.0, NEG)
        gates = g_ref[...]
        for k in range(N_KV):
            qk = q_heads(k)

            def new_rows(kind, k=k):
                c0 = COL_KV + kind * KV_WIDTH + k * HEAD_DIM
                return zs_ref[:, c0:c0 + HEAD_DIM].astype(BF16)

            _flash_update(m_sc, l_sc, acc_sc, k, 0, rows, _dot_nt(qk, new_rows(2)) + new_bias, new_rows(3))
            o_slc = acc_sc[k] * (1.0 / l_sc[k])
            kw = win_rows[pl.ds(k, n_win, stride=2 * N_KV), :].astype(BF16)
            vw = win_rows[pl.ds(N_KV + k, n_win, stride=2 * N_KV), :].astype(BF16)
            s1 = _dot_nt(qk, kw) + win_bias
            s2 = _dot_nt(qk, new_rows(4)) + new_bias
            mx = jnp.maximum(jnp.max(s1, axis=1, keepdims=True), jnp.max(s2, axis=1, keepdims=True))
            p1 = jnp.exp(s1 - mx)
            p2 = jnp.exp(s2 - mx)
            den = jnp.sum(p1, axis=1, keepdims=True) + jnp.sum(p2, axis=1, keepdims=True)
            o_win = (_dot(p1.astype(BF16), vw) + _dot(p2.astype(BF16), new_rows(5))) * (1.0 / den)
            o_cmp = ocmp_sc[k]
            for g in range(GROUP):
                r = slice(g * tp, (g + 1) * tp)
                c = k * LANES
                o = (gates[:, c + g:c + g + 1] * o_cmp[r]
                     + gates[:, c + GROUP + g:c + GROUP + g + 1] * o_slc[r]
                     + gates[:, c + 2 * GROUP + g:c + 2 * GROUP + g + 1] * o_win[r])
                h = k * GROUP + g
                o_ref[:, h * HEAD_DIM:(h + 1) * HEAD_DIM] = o.astype(o_ref.dtype)


def _attn_sample(c5, layer, page_table, zs, gates, kc, vc, win5, t_new, n_pages):
    bsz, tot_pages = page_table.shape
    n_groups = tot_pages // n_pages
    past_len = tot_pages * PAGE
    n_slc = -(-(past_len + t_new) // SLC_LEN)
    sel_lanes = -(-n_slc // LANES) * LANES
    n_win = win5.shape[2]
    tp = SUBLANES
    rows = GROUP * tp
    page_specs = [pl.BlockSpec((None, None, PAGE, 2 * N_KV, HEAD_DIM),
                               functools.partial(lambda b, g, pt, i: (layer, pt[b, g * n_pages + i], 0, 1, 0), i=i))
                  for i in range(n_pages)]

    def bmap(b, g, pt):
        return (b, 0, 0)

    grid_spec = pltpu.PrefetchScalarGridSpec(
        num_scalar_prefetch=1,
        grid=(bsz, n_groups),
        in_specs=page_specs + [
            pl.BlockSpec((None, tp, zs.shape[2]), bmap),
            pl.BlockSpec((None, kc.shape[1], KV_WIDTH), bmap),
            pl.BlockSpec((None, vc.shape[1], KV_WIDTH), bmap),
            pl.BlockSpec((None, None, n_win, 2 * N_KV, HEAD_DIM), lambda b, g, pt: (layer, b, 0, 0, 0)),
            pl.BlockSpec((None, tp, N_KV * LANES), bmap)],
        out_specs=pl.BlockSpec((None, tp, ATTN_WIDTH), bmap),
        scratch_shapes=[pltpu.VMEM((N_KV, rows, HEAD_DIM), F32),
                        pltpu.VMEM((N_KV, tp, sel_lanes), F32),
                        pltpu.VMEM((N_KV, rows, LANES), F32),
                        pltpu.VMEM((N_KV, rows, LANES), F32),
                        pltpu.VMEM((N_KV, rows, HEAD_DIM), F32)],
    )
    return pl.pallas_call(
        functools.partial(_attn_sample_kernel, n_pages=n_pages, n_groups=n_groups, t_new=t_new,
                          n_slc=n_slc, n_win=n_win),
        out_shape=jax.ShapeDtypeStruct((bsz, tp, ATTN_WIDTH), BF16),
        grid_spec=grid_spec,
        compiler_params=_cparams(("parallel", "arbitrary")),
        name="attn_sample",
    )(page_table, *([c5] * n_pages), zs, kc, vc, win5, gates)


CONV_HALO = 32
CONV_LANE_CHUNK = 256


def _conv_kernel(*refs, tc, multi_tile):
    if multi_tile:
        u_ref, up_ref, buf_ref, dw_ref, db_ref, lg_ref, lb_ref, o_ref, tail_ref, ext_sc, y_sc = refs
    else:
        u_ref, buf_ref, dw_ref, db_ref, lg_ref, lb_ref, o_ref, tail_ref, ext_sc, y_sc = refs
    t = pl.program_id(1)

    def glu(u):
        return u[:, :CONV_CH] * jax.nn.sigmoid(u[:, CONV_CH:])

    @pl.when(t == 0)
    def _():
        ext_sc[0:CONV_HALO, :] = buf_ref[...]

    if multi_tile:
        @pl.when(t > 0)
        def _():
            ext_sc[0:CONV_HALO, :] = glu(up_ref[...])

    ext_sc[CONV_HALO:CONV_HALO + tc, :] = glu(u_ref[...])
    off = CONV_HALO - (CONV_WIDTH - 1)
    n_ext = CONV_HALO + tc
    for c0 in range(0, CONV_CH, CONV_LANE_CHUNK):
        cs = slice(c0, c0 + CONV_LANE_CHUNK)
        ext = ext_sc[:, cs]
        shifted = [ext] + [pltpu.roll(ext, n_ext - s, 0) for s in range(1, SUBLANES)]
        acc = jnp.broadcast_to(db_ref[:, cs], (tc, CONV_LANE_CHUNK))
        for w in range(CONV_WIDTH):
            s = (off + w) % SUBLANES
            a = off + w - s
            acc = acc + shifted[s][a:a + tc] * dw_ref[w:w + 1, cs]
        y_sc[:, cs] = acc
    y = y_sc[...]
    mu = jnp.mean(y, axis=-1, keepdims=True)
    d = y - mu
    var = jnp.mean(d * d, axis=-1, keepdims=True)
    yn = d * lax.rsqrt(var + LN_EPS) * lg_ref[...] + lb_ref[...]
    o_ref[...] = (yn * jax.nn.sigmoid(yn)).astype(o_ref.dtype)

    @pl.when(t == pl.num_programs(1) - 1)
    def _():
        tail_ref[...] = ext_sc[tc:tc + CONV_HALO, :]


def _conv_group(z, buf, dw_w, dw_b, ln_g, ln_b, tc):
    bsz, t_len, _ = z.shape
    n_t = t_len // tc
    multi_tile = n_t > 1
    assert COL_U == 0 and (not multi_tile or tc % CONV_HALO == 0)
    halo_per_tile = max(tc // CONV_HALO, 1)

    def vec(v):
        return v.reshape(1, CONV_CH)

    cmap = lambda b, t: (0, 0)
    in_specs = [pl.BlockSpec((None, tc, 2 * CONV_CH), lambda b, t: (b, t, 0))]
    args = [z]
    if multi_tile:
        in_specs.append(pl.BlockSpec((None, CONV_HALO, 2 * CONV_CH),
                                     lambda b, t: (b, jnp.maximum(t * halo_per_tile - 1, 0), 0)))
        args.append(z)
    in_specs += [pl.BlockSpec((None, CONV_HALO, CONV_CH), lambda b, t: (b, 0, 0)),
                 pl.BlockSpec((CONV_WIDTH, CONV_CH), cmap),
                 pl.BlockSpec((1, CONV_CH), cmap), pl.BlockSpec((1, CONV_CH), cmap),
                 pl.BlockSpec((1, CONV_CH), cmap)]
    args += [buf, dw_w, vec(dw_b), vec(ln_g), vec(ln_b)]
    return pl.pallas_call(
        functools.partial(_conv_kernel, tc=tc, multi_tile=multi_tile),
        out_shape=(jax.ShapeDtypeStruct((bsz, t_len, CONV_CH), BF16),
                   jax.ShapeDtypeStruct((bsz, CONV_HALO, CONV_CH), F32)),
        grid=(bsz, n_t),
        in_specs=in_specs,
        out_specs=(pl.BlockSpec((None, tc, CONV_CH), lambda b, t: (b, t, 0)),
                   pl.BlockSpec((None, CONV_HALO, CONV_CH), lambda b, t: (b, 0, 0))),
        scratch_shapes=[pltpu.VMEM((CONV_HALO + tc, CONV_CH), F32), pltpu.VMEM((tc, CONV_CH), F32)],
        compiler_params=_cparams(("parallel", "arbitrary")),
        name="conv_group",
    )(*args)


def _rope_tables(pos):
    half = ROT_DIM // 2
    inv = ROPE_THETA ** (-jnp.arange(half, dtype=F32) / half)
    ang = pos.astype(F32)[:, None] * inv[None, :]
    cos, sin = jnp.cos(ang), jnp.sin(ang)
    n = pos.shape[0]
    zeros = jnp.zeros((n, half), F32)
    rest = HEAD_DIM - ROT_DIM
    c = jnp.concatenate([cos, cos, jnp.ones((n, rest), F32)], axis=1)
    s1 = jnp.concatenate([zeros, sin, jnp.zeros((n, rest), F32)], axis=1)
    s2 = jnp.concatenate([-sin, zeros, jnp.zeros((n, rest), F32)], axis=1)
    return c, s1, s2


def _prep_in_weights(w_in):
    depth, d, _ = w_in.shape
    kv_end = ATTN_WIDTH + KV_COLS
    gate_end = kv_end + 3 * N_HEADS
    w_main = jnp.concatenate([w_in[:, :, gate_end:], w_in[:, :, :kv_end]], axis=2).astype(BF16)
    wg = w_in[:, :, kv_end:gate_end].reshape(depth, d, 3, N_KV, GROUP).transpose(0, 1, 3, 2, 4)
    wg = jnp.pad(wg.reshape(depth, d, N_KV, 3 * GROUP), ((0, 0), (0, 0), (0, 0), (0, LANES - 3 * GROUP)))
    return w_main, wg.reshape(depth, d, N_KV * LANES).astype(BF16)


def _prep_cmp_weights(cmp_pe_l, cmp_w1_l, cmp_w2_l, cmp_b2_l):
    pef = jnp.pad(cmp_pe_l.reshape(2, 1, CMP_LEN * HEAD_DIM), ((0, 0), (0, SUBLANES - 1), (0, 0))).astype(BF16)
    w1 = cmp_w1_l.astype(BF16).reshape(2, CMP_LEN // CMP_STRIDE, CMP_STRIDE * HEAD_DIM, CMP_HIDDEN)
    return pef, w1, cmp_w2_l.astype(BF16), cmp_b2_l.reshape(2, 1, HEAD_DIM)


def _largest_tile(n, unit, cap):
    best = None
    for t in range(unit, min(n, cap) + 1, unit):
        if n % t == 0:
            best = t
    assert best is not None, (n, unit, cap)
    return best


def _dense_tail(x, attn, conv, layer, w_out_b, norm_ffn_l, w_gate_b, w_up_b, w_down_b, tm):
    d = x.shape[1]
    d_ff = w_gate_b.shape[2]
    h = _outproj(attn, conv, w_out_b, layer, x, tm, 512)
    hn = _rmsnorm(h, norm_ffn_l, BF16)
    act = _ffn_up(hn, w_gate_b, w_up_b, layer, tm, _largest_tile(d_ff, 2 * LANES, 512))
    return _ffn_down(act, w_down_b, layer, h, min(tm, 512), min(d, 1024), _largest_tile(d_ff, LANES, 6144))


def kernel(x_prompt, x_sample, cache_kv, cache_win, state_conv, page_table, norm_mix, w_in, cmp_pe, cmp_w1,
           cmp_w2, cmp_b2, conv_dw_w, conv_dw_b, conv_ln_g, conv_ln_b, w_out, norm_ffn, w_gate, w_up, w_down,
           norm_final):
    depth = w_in.shape[0]
    bsz, seq, d = x_prompt.shape
    dec_b, dec_t, _ = x_sample.shape
    n_pool, page_size = cache_kv.shape[1], cache_kv.shape[2]
    tot_pages = page_table.shape[1]
    past_len = tot_pages * page_size
    assert page_size == PAGE and seq % PAGE == 0 and dec_t <= SUBLANES
    assert conv_dw_w.shape[2] == CONV_CH and d == ATTN_WIDTH + CONV_CH
    pages_per_group = min(16, tot_pages, seq // PAGE)
    assert tot_pages % pages_per_group == 0 and (seq // PAGE) % pages_per_group == 0
    tp = SUBLANES
    ms = dec_b * tp

    tm_p = min(1024, seq)
    tabs_p = _rope_tables(jnp.arange(seq, dtype=I32))
    tabs_s = tuple(jnp.tile(t, (dec_b, 1)) for t in _rope_tables(past_len + jnp.arange(tp, dtype=I32)))

    hp = x_prompt.reshape(bsz * seq, d)
    hs = jnp.pad(x_sample, ((0, 0), (0, tp - dec_t), (0, 0))).reshape(ms, d)
    cache5 = cache_kv.reshape(depth, n_pool, page_size, 4 * N_KV, HEAD_DIM)
    win5 = cache_win.reshape(depth, dec_b, cache_win.shape[2], 2 * N_KV, HEAD_DIM)
    zero_buf = jnp.zeros((bsz, CONV_HALO, CONV_CH), F32)
    tm_dense = min(1024, bsz * seq)

    w_main, wg = _prep_in_weights(w_in)
    w_out_b, w_gate_b, w_up_b, w_down_b = (w.astype(BF16) for w in (w_out, w_gate, w_up, w_down))

    z_layers, convp, kvs, wins, convs = [], [], [], [], []
    for l in range(depth):
        cw = _prep_cmp_weights(cmp_pe[l], cmp_w1[l], cmp_w2[l], cmp_b2[l])

        xn = _rmsnorm(hp, norm_mix[l], BF16)
        z, zc = _inproj(xn, w_main, l, tabs_p, seq // tm_p, tm_p)
        gates = _gates(xn, wg, l, tm_p)
        z_layers.append(z)
        z3 = z.reshape(bsz, seq, z.shape[1])
        kc, vc = _compress_prompt(zc.reshape(bsz, seq, 2 * N_KV, HEAD_DIM), cw, pages_per_group)
        attn = _attn_prompt(z3, gates.reshape(bsz, seq, N_KV * LANES), kc, vc)
        conv, tail = _conv_group(z3, zero_buf, conv_dw_w[l], conv_dw_b[l], conv_ln_g[l], conv_ln_b[l], PAGE)
        hp = _dense_tail(hp, attn.reshape(bsz * seq, ATTN_WIDTH), conv.reshape(bsz * seq, CONV_CH), l, w_out_b,
                         norm_ffn[l], w_gate_b, w_up_b, w_down_b, tm_dense)
        convp.append(tail[:, CONV_HALO - (CONV_WIDTH - 1):])

        xn_s = _rmsnorm(hs, norm_mix[l], BF16)
        zs, _ = _inproj(xn_s, w_main, l, tabs_s, 1, ms)
        gates_s = _gates(xn_s, wg, l, ms)
        zs3 = zs.reshape(dec_b, tp, zs.shape[1])
        kc_s, vc_s = _compress_sample(cache5, l, page_table, cw, pages_per_group)
        attn_s = _attn_sample(cache5, l, page_table, zs3, gates_s.reshape(dec_b, tp, N_KV * LANES),
                              kc_s, vc_s, win5, dec_t, pages_per_group)
        buf_s = jnp.pad(state_conv[l], ((0, 0), (CONV_HALO - (CONV_WIDTH - 1), 0), (0, 0)))
        conv_s, tail_s = _conv_group(zs3, buf_s, conv_dw_w[l], conv_dw_b[l], conv_ln_g[l], conv_ln_b[l], tp)
        hs = _dense_tail(hs, attn_s.reshape(ms, ATTN_WIDTH), conv_s.reshape(ms, CONV_CH), l, w_out_b, norm_ffn[l],
                         w_gate_b, w_up_b, w_down_b, ms)
        kvs.append(zs3[:, :dec_t, COL_KV:COL_KV + 4 * KV_WIDTH].reshape(dec_b, dec_t, 4, N_KV, HEAD_DIM))
        new_win_s = zs3[:, :dec_t, COL_KV + 4 * KV_WIDTH:COL_KV + KV_COLS].reshape(dec_b, dec_t, 2, N_KV, HEAD_DIM)
        win_all = jnp.concatenate([cache_win[l], new_win_s], axis=1)
        wins.append(win_all[:, win_all.shape[1] - min(WINDOW, win_all.shape[1]):])
        new_glu = tail_s[:, CONV_HALO - tp:CONV_HALO - tp + dec_t]
        ext_s = jnp.concatenate([state_conv[l], new_glu], axis=1)
        convs.append(ext_s[:, ext_s.shape[1] - (CONV_WIDTH - 1):])

    y_prompt = _rmsnorm(hp, norm_final, F32).reshape(bsz, seq, d)
    y_sample = _rmsnorm(hs, norm_final, F32).reshape(dec_b, tp, d)[:, :dec_t]
    keep = min(WINDOW, seq)
    tiles_per_batch = seq // keep
    assert seq % keep == 0
    kv_rows = _rows_out(z_layers, bsz * tiles_per_batch, keep, COL_KV, 4 * N_KV, lambda i: i)
    win_rows = _rows_out(z_layers, bsz, keep, COL_KV + 4 * KV_WIDTH, 2 * N_KV,
                         lambda i: i * tiles_per_batch + tiles_per_batch - 1)
    kv_prompt = kv_rows.reshape(depth, bsz, seq, 4, N_KV, HEAD_DIM)
    win_prompt = win_rows.reshape(depth, bsz, keep, 2, N_KV, HEAD_DIM)
    return (y_prompt, y_sample, kv_prompt, win_prompt, jnp.stack(convp),
            jnp.stack(kvs), jnp.stack(wins), jnp.stack(convs))
```

```python
import functools
import math

import jax
import jax.numpy as jnp
from jax import lax
from jax.experimental import pallas as pl
from jax.experimental.pallas import tpu as pltpu

F32 = jnp.float32
BF16 = jnp.bfloat16
I32 = jnp.int32

N_HEADS = 16
N_KV = 4
HEAD_DIM = 128
GROUP = N_HEADS // N_KV
ATTN_WIDTH = N_HEADS * HEAD_DIM
KV_WIDTH = N_KV * HEAD_DIM
CONV_CH = 2048
CONV_WIDTH = 31
ROT_DIM = HEAD_DIM // 4
ROPE_THETA = 500000.0
CMP_LEN = 32
CMP_STRIDE = 16
CMP_HIDDEN = 2 * HEAD_DIM
SLC_LEN = 64
SLC_SHIFT = 6
N_SEL = 16
WINDOW = 512
SCALE = HEAD_DIM ** -0.5
LOG2E = 1.0 / math.log(2.0)
SLC_KEY_BLOCK = 512
NEG = -1e30
FORCE = 1e9
RMS_EPS = 1e-6
LN_EPS = 1e-5

LANES = 128
SUBLANES = 8
VMEM_LIMIT = 56 * 1024 * 1024

PAGE = 128
CHUNKS_PER_PAGE = PAGE // CMP_STRIDE
COL_U = 0
COL_Q = 2 * CONV_CH
COL_KV = COL_Q + ATTN_WIDTH
KV_COLS = 6 * KV_WIDTH
PROJ_TN = 512


def _cparams(sem):
    return pltpu.CompilerParams(dimension_semantics=sem, vmem_limit_bytes=VMEM_LIMIT)


def _dot(a, b):
    return jnp.dot(a, b, preferred_element_type=F32)


def _dot_nt(a, b):
    return lax.dot_general(a, b, (((1,), (1,)), ((), ())), preferred_element_type=F32)


def _split3_dot(x, m_bf16):
    hi = x.astype(BF16)
    r1 = x - hi.astype(F32)
    mid = r1.astype(BF16)
    lo = (r1 - mid.astype(F32)).astype(BF16)
    return _dot(hi, m_bf16) + _dot(mid, m_bf16) + _dot(lo, m_bf16)


def _rmsnorm_kernel(x_ref, g_ref, o_ref):
    x = x_ref[...]
    ms = jnp.mean(x * x, axis=-1, keepdims=True)
    o_ref[...] = (x * lax.rsqrt(ms + RMS_EPS) * g_ref[...]).astype(o_ref.dtype)


def _rmsnorm(x, g, out_dtype):
    m, d = x.shape
    tm = min(m, 256)
    return pl.pallas_call(
        _rmsnorm_kernel,
        out_shape=jax.ShapeDtypeStruct((m, d), out_dtype),
        grid=(m // tm,),
        in_specs=[pl.BlockSpec((tm, d), lambda i: (i, 0)),
                  pl.BlockSpec((1, d), lambda i: (0, 0))],
        out_specs=pl.BlockSpec((tm, d), lambda i: (i, 0)),
        compiler_params=_cparams(("parallel",)),
        name="rmsnorm",
    )(x, g.reshape(1, d))


def _inproj_kernel(a_ref, w_ref, c_ref, s1_ref, s2_ref, z_ref, zc_ref):
    j = pl.program_id(1)
    n_heads_tile = PROJ_TN // HEAD_DIM
    acc = _dot(a_ref[...], w_ref[...])
    q0 = COL_Q // PROJ_TN
    k0 = COL_KV // PROJ_TN
    is_rope = ((j >= q0) & (j <= k0)) | (j == k0 + 2) | (j == k0 + 4)

    @pl.when(is_rope)
    def _():
        c = c_ref[...]
        s1 = s1_ref[...]
        s2 = s2_ref[...]
        rotated = []
        for h in range(n_heads_tile):
            x = acc[:, h * HEAD_DIM:(h + 1) * HEAD_DIM]
            rotated.append(x * c + pltpu.roll(x, ROT_DIM // 2, 1) * s1
                           + pltpu.roll(x, HEAD_DIM - ROT_DIM // 2, 1) * s2)
            z_ref[:, h * HEAD_DIM:(h + 1) * HEAD_DIM] = rotated[h]

        @pl.when(j == k0)
        def _():
            for h in range(n_heads_tile):
                zc_ref[:, h, :] = rotated[h]

    @pl.when(jnp.logical_not(is_rope))
    def _():
        z_ref[...] = acc

        @pl.when(j == k0 + 1)
        def _():
            for h in range(n_heads_tile):
                zc_ref[:, n_heads_tile + h, :] = acc[:, h * HEAD_DIM:(h + 1) * HEAD_DIM]


def _inproj(xn, w, layer, tabs, tab_tiles, tm):
    m, k = xn.shape
    n = w.shape[2]
    c, s1, s2 = tabs
    tab_spec = pl.BlockSpec((tm, HEAD_DIM), lambda i, j: (i % tab_tiles, 0))
    return pl.pallas_call(
        _inproj_kernel,
        out_shape=(jax.ShapeDtypeStruct((m, n), F32),
                   jax.ShapeDtypeStruct((m, 2 * N_KV, HEAD_DIM), F32)),
        grid=(m // tm, n // PROJ_TN),
        in_specs=[pl.BlockSpec((tm, k), lambda i, j: (i, 0)),
                  pl.BlockSpec((None, k, PROJ_TN), lambda i, j: (layer, 0, j)),
                  tab_spec, tab_spec, tab_spec],
        out_specs=(pl.BlockSpec((tm, PROJ_TN), lambda i, j: (i, j)),
                   pl.BlockSpec((tm, 2 * N_KV, HEAD_DIM), lambda i, j: (i, 0, 0))),
        compiler_params=_cparams(("parallel", "arbitrary")),
        name="inproj",
    )(xn, w, c, s1, s2)


def _norm_gates_kernel(x_ref, gn_ref, wg_ref, xn_ref, g_ref):
    x = x_ref[...]
    ms = jnp.mean(x * x, axis=-1, keepdims=True)
    xn = (x * lax.rsqrt(ms + RMS_EPS) * gn_ref[...]).astype(xn_ref.dtype)
    xn_ref[...] = xn
    g_ref[...] = jax.nn.sigmoid(_dot(xn, wg_ref[...]))


def _norm_gates(x, g_norm, wg, layer):
    m, d = x.shape
    n = wg.shape[2]
    tm = min(m, 256)
    return pl.pallas_call(
        _norm_gates_kernel,
        out_shape=(jax.ShapeDtypeStruct((m, d), BF16), jax.ShapeDtypeStruct((m, n), F32)),
        grid=(m // tm,),
        in_specs=[pl.BlockSpec((tm, d), lambda i: (i, 0)),
                  pl.BlockSpec((1, d), lambda i: (0, 0)),
                  pl.BlockSpec((None, d, n), lambda i: (layer, 0, 0))],
        out_specs=(pl.BlockSpec((tm, d), lambda i: (i, 0)), pl.BlockSpec((tm, n), lambda i: (i, 0))),
        compiler_params=_cparams(("parallel",)),
        name="norm_gates",
    )(x, g_norm.reshape(1, d), wg)


def _rows_out_kernel(*refs, depth):
    z_refs = refs[:depth]
    o_ref = refs[depth]
    layer = pl.program_id(0)
    for li, z_ref in enumerate(z_refs):
        @pl.when(layer == li)
        def _(z_ref=z_ref):
            for r in range(o_ref.shape[1]):
                o_ref[:, r, :] = z_ref[:, r * HEAD_DIM:(r + 1) * HEAD_DIM]


def _rows_out(zs, n_tiles, tm, col0, n_head_rows, row_tile_of):
    depth = len(zs)
    width = n_head_rows * HEAD_DIM
    assert col0 % width == 0
    in_specs = [pl.BlockSpec((tm, width),
                             functools.partial(lambda l, i, li: (jnp.where(l == li, row_tile_of(i), 0), col0 // width),
                                               li=li))
                for li in range(depth)]
    return pl.pallas_call(
        functools.partial(_rows_out_kernel, depth=depth),
        out_shape=jax.ShapeDtypeStruct((depth, n_tiles * tm, n_head_rows, HEAD_DIM), F32),
        grid=(depth, n_tiles),
        in_specs=in_specs,
        out_specs=pl.BlockSpec((None, tm, n_head_rows, HEAD_DIM), lambda l, i: (l, i, 0, 0)),
        compiler_params=_cparams(("arbitrary", "arbitrary")),
        name="rows_out",
    )(*zs)


def _outproj_kernel(a1_ref, a2_ref, w1_ref, w2_ref, r_ref, o_ref):
    o_ref[...] = r_ref[...] + _dot(a1_ref[...], w1_ref[...]) + _dot(a2_ref[...], w2_ref[...])


def _outproj(a1, a2, w, layer, res, tm, tn):
    m, k1 = a1.shape
    k2 = a2.shape[1]
    n = w.shape[2]
    assert k1 == k2 and w.shape[1] == k1 + k2
    return pl.pallas_call(
        _outproj_kernel,
        out_shape=jax.ShapeDtypeStruct((m, n), F32),
        grid=(m // tm, n // tn),
        in_specs=[pl.BlockSpec((tm, k1), lambda i, j: (i, 0)),
                  pl.BlockSpec((tm, k2), lambda i, j: (i, 0)),
                  pl.BlockSpec((None, k1, tn), lambda i, j: (layer, 0, j)),
                  pl.BlockSpec((None, k2, tn), lambda i, j: (layer, 1, j)),
                  pl.BlockSpec((tm, tn), lambda i, j: (i, j))],
        out_specs=pl.BlockSpec((tm, tn), lambda i, j: (i, j)),
        compiler_params=_cparams(("parallel", "arbitrary")),
        name="outproj",
    )(a1, a2, w, w, res)


def _ffn_up_kernel(a_ref, wg_ref, wu_ref, o_ref):
    a = a_ref[...]
    gate = _dot(a, wg_ref[...])
    up = _dot(a, wu_ref[...])
    o_ref[...] = (gate * jax.nn.sigmoid(gate) * up).astype(o_ref.dtype)


def _ffn_up(a, wg, wu, layer, tm, tn):
    m, k = a.shape
    n = wg.shape[2]
    return pl.pallas_call(
        _ffn_up_kernel,
        out_shape=jax.ShapeDtypeStruct((m, n), BF16),
        grid=(m // tm, n // tn),
        in_specs=[pl.BlockSpec((tm, k), lambda i, j: (i, 0)),
                  pl.BlockSpec((None, k, tn), lambda i, j: (layer, 0, j)),
                  pl.BlockSpec((None, k, tn), lambda i, j: (layer, 0, j))],
        out_specs=pl.BlockSpec((tm, tn), lambda i, j: (i, j)),
        compiler_params=_cparams(("parallel", "arbitrary")),
        name="ffn_up",
    )(a, wg, wu)


def _ffn_down_kernel(a_ref, w_ref, r_ref, o_ref, acc_ref):
    kk = pl.program_id(2)

    @pl.when(kk == 0)
    def _():
        acc_ref[...] = r_ref[...]

    acc_ref[...] += _dot(a_ref[...], w_ref[...])

    @pl.when(kk == pl.num_programs(2) - 1)
    def _():
        o_ref[...] = acc_ref[...]


def _ffn_down(a, w, layer, res, tm, tn, tk):
    m, k = a.shape
    n = w.shape[2]
    return pl.pallas_call(
        _ffn_down_kernel,
        out_shape=jax.ShapeDtypeStruct((m, n), F32),
        grid=(m // tm, n // tn, k // tk),
        in_specs=[pl.BlockSpec((tm, tk), lambda i, j, kk: (i, kk)),
                  pl.BlockSpec((None, tk, tn), lambda i, j, kk: (layer, kk, j)),
                  pl.BlockSpec((tm, tn), lambda i, j, kk: (i, j))],
        out_specs=pl.BlockSpec((tm, tn), lambda i, j, kk: (i, j)),
        scratch_shapes=[pltpu.VMEM((tm, tn), F32)],
        compiler_params=_cparams(("parallel", "parallel", "arbitrary")),
        name="ffn_down",
    )(a, w, res)


def _gelu_tanh(x):
    return 0.5 * x * (1.0 + jnp.tanh(math.sqrt(2.0 / math.pi) * (x + 0.044715 * (x * x * x))))


def _compress_kernel(*refs, n_pages, n_prefetch):
    page_refs = refs[n_prefetch:n_prefetch + n_pages]
    pef_ref, w1_ref, w2_ref, b2_ref, kc_ref, vc_ref, carry_ref = refs[n_prefetch + n_pages:]
    grp = pl.program_id(1)
    n_chunks = n_pages * CHUNKS_PER_PAGE
    half_k = CMP_STRIDE * HEAD_DIM

    @pl.when(grp == 0)
    def _():
        carry_ref[...] = jnp.zeros_like(carry_ref)

    rows = N_KV * n_chunks
    pairs = CHUNKS_PER_PAGE // 2
    lo = lax.broadcasted_iota(I32, (pairs, CMP_STRIDE, 2 * N_KV, HEAD_DIM), 2) < N_KV
    xk, xv = [], []
    for p in page_refs:
        x = p[...].reshape(pairs, 2, CMP_STRIDE, 2 * N_KV, HEAD_DIM)
        even, odd = x[:, 0], x[:, 1]
        kp = jnp.where(lo, even, pltpu.roll(odd, N_KV, 2))
        vp = jnp.where(lo, pltpu.roll(even, N_KV, 2), odd)
        for t, dst in ((kp, xk), (vp, xv)):
            dst.append(jnp.concatenate([t[:, j] for j in range(CMP_STRIDE)], axis=-1)
                       .reshape(pairs * 2 * N_KV, CMP_STRIDE * HEAD_DIM))
    row = lax.broadcasted_iota(I32, (rows, CMP_HIDDEN), 0)
    for kind, parts, out_ref in ((0, xk, kc_ref), (1, xv, vc_ref)):
        x_all = jnp.concatenate(parts, axis=0).astype(BF16)
        a0 = _dot(x_all, w1_ref[kind, 0])
        a1 = _dot(x_all, w1_ref[kind, 1])
        pe = pef_ref[kind]
        pe_term = (_dot(pe[:, :half_k], w1_ref[kind, 0]) + _dot(pe[:, half_k:], w1_ref[kind, 1]))[0:1, :]
        shifted = pltpu.roll(jnp.where(row >= rows - N_KV, carry_ref[kind], a0), N_KV, 0)
        pre = shifted + a1 + pe_term
        out_ref[...] = (_dot(_gelu_tanh(pre).astype(BF16), w2_ref[kind]) + b2_ref[kind]).astype(out_ref.dtype)
        carry_ref[kind] = a0


def _compress_prompt(zc, cw, n_pages):
    bsz, t_len = zc.shape[:2]
    specs = [pl.BlockSpec((None, PAGE, 2 * N_KV, HEAD_DIM),
                          functools.partial(lambda b, g, i: (b, g * n_pages + i, 0, 0), i=i))
             for i in range(n_pages)]
    return _compress(zc, specs, bsz, (t_len // PAGE) // n_pages, n_pages, cw)


def _compress_sample(c5, layer, page_table, cw, n_pages):
    bsz, tot_pages = page_table.shape
    specs = [pl.BlockSpec((None, None, PAGE, 2 * N_KV, HEAD_DIM),
                          functools.partial(lambda b, g, pt, i: (layer, pt[b, g * n_pages + i], 0, 0, 0), i=i))
             for i in range(n_pages)]
    return _compress(c5, specs, bsz, tot_pages // n_pages, n_pages, cw, prefetch=(page_table,))


def _compress(src, page_specs, n_batch, n_groups, n_pages, cw, prefetch=()):
    pef, w1, w2, b2 = cw
    n_chunks = n_pages * CHUNKS_PER_PAGE

    def const_spec(x):
        return pl.BlockSpec(x.shape, lambda b, g, *pt: (0,) * x.ndim)

    out_spec = pl.BlockSpec((None, N_KV * n_chunks, HEAD_DIM), lambda b, g, *pt: (b, g, 0))
    out_shape = jax.ShapeDtypeStruct((n_batch, n_groups * N_KV * n_chunks, HEAD_DIM), BF16)
    grid_spec = pltpu.PrefetchScalarGridSpec(
        num_scalar_prefetch=len(prefetch),
        grid=(n_batch, n_groups),
        in_specs=list(page_specs) + [const_spec(pef), const_spec(w1), const_spec(w2), const_spec(b2)],
        out_specs=(out_spec, out_spec),
        scratch_shapes=[pltpu.VMEM((2, N_KV * n_chunks, CMP_HIDDEN), F32)],
    )
    kc, vc = pl.pallas_call(
        functools.partial(_compress_kernel, n_pages=n_pages, n_prefetch=len(prefetch)),
        out_shape=(out_shape, out_shape),
        grid_spec=grid_spec,
        compiler_params=_cparams(("parallel", "arbitrary")),
        name="compress",
    )(*prefetch, *([src] * n_pages), pef, w1, w2, b2)
    return (kc.reshape(n_batch, n_groups * n_chunks, KV_WIDTH), vc.reshape(n_batch, n_groups * n_chunks, KV_WIDTH))


def _flash_init(m_sc, l_sc, acc_sc, br):
    m_sc[br] = jnp.full(m_sc.shape[1:], -jnp.inf, F32)
    l_sc[br] = jnp.zeros(l_sc.shape[1:], F32)
    acc_sc[br] = jnp.zeros(acc_sc.shape[1:], F32)


def _flash_update(m_sc, l_sc, acc_sc, br, r0, nr, s, v):
    m_old = m_sc[br, r0:r0 + nr, :]
    m_new = jnp.maximum(m_old, jnp.max(s, axis=1, keepdims=True))
    alpha = jnp.exp(m_old - m_new)
    p = jnp.exp(s - (m_new if s.shape[1] == LANES else m_new[:, 0:1]))
    l_sc[br, r0:r0 + nr, :] = alpha * l_sc[br, r0:r0 + nr, :] + jnp.sum(p, axis=1, keepdims=True)
    acc_sc[br, r0:r0 + nr, :] = alpha * acc_sc[br, r0:r0 + nr, :] + _dot(p.astype(BF16), v)
    m_sc[br, r0:r0 + nr, :] = m_new


def _overlap_matrix(shape, n_slc):
    m = lax.broadcasted_iota(I32, shape, 0)
    j = lax.broadcasted_iota(I32, shape, 1)
    start = (m - 1) * CMP_STRIDE
    ov = (m >= 1) & (start < j * SLC_LEN + SLC_LEN) & (start + CMP_LEN > j * SLC_LEN) & (j < n_slc)
    return jnp.where(ov, 1.0, 0.0).astype(BF16)


def _lane_tile_reduce(x, op):
    out = x[..., 0:LANES]
    for i in range(1, x.shape[-1] // LANES):
        out = op(out, x[..., i * LANES:(i + 1) * LANES])
    return out


def _attn_prompt_kernel(q_ref, kc_ref, vc_ref, ks_ref, vs_ref, kw_ref, vw_ref, g_ref, o_ref,
                        ksb_sc, vsb_sc, kwb_sc, vwb_sc, s_sc, m_sc, l_sc, acc_sc, *, n_slc, kb, win_keys):
    qt = pl.program_id(2)
    tq = PAGE
    rows = GROUP * tq
    t_len = ks_ref.shape[0]

    @pl.when(qt == 0)
    def _():
        ksb_sc[...] = ks_ref[...].astype(BF16)
        vsb_sc[...] = vs_ref[...].astype(BF16)
        kwb_sc[...] = kw_ref[...].astype(BF16)
        vwb_sc[...] = vw_ref[...].astype(BF16)

    q = q_ref[...] * (SCALE * LOG2E)
    q_all = jnp.concatenate([q[:, g * HEAD_DIM:(g + 1) * HEAD_DIM] for g in range(GROUP)],
                            axis=0).astype(BF16)
    row = lax.broadcasted_iota(I32, (tq, LANES), 0)
    lane = lax.broadcasted_iota(I32, (tq, LANES), 1)
    pos = qt * tq + row

    valid = (lane >= 1) & (lane * CMP_STRIDE + (CMP_LEN - CMP_STRIDE - 1) <= pos)
    vbias = jnp.where(valid, 0.0, NEG)
    vf = jnp.where(valid, 1.0, 0.0)
    s3 = _dot_nt(q_all, kc_ref[...]).reshape(GROUP, tq, LANES) + vbias[None]
    e = jnp.exp2(s3 - jnp.max(s3, axis=2, keepdims=True)) * vf[None]
    denom = jnp.sum(e, axis=2, keepdims=True)
    p3 = e * (1.0 / jnp.maximum(denom, 1e-30))
    o_cmp = _dot(p3.reshape(rows, LANES).astype(BF16), vc_ref[...])
    psum = p3[0] + p3[1] + p3[2] + p3[3]

    imp = _split3_dot(psum, _overlap_matrix((LANES, LANES), n_slc))
    cur = pos >> SLC_SHIFT
    forced = (lane == 0) | (lane == cur) | (lane == cur - 1)
    score = jnp.where(lane <= cur, jnp.where(forced, FORCE, imp), -1.0)
    score = jnp.where(lane < n_slc, score, -2.0)
    n_rank = -(-n_slc // SUBLANES) * SUBLANES
    score_t = score.T[0:n_rank]
    jrow = lax.broadcasted_iota(I32, (n_rank, tq), 0)
    rank_t = jnp.zeros((n_rank, tq), F32)
    for jp in range(n_slc):
        other = score_t[jp:jp + 1, :]
        tie = jnp.where(jrow > jp, 1.0, 0.0)
        rank_t = rank_t + jnp.where(other > score_t, 1.0, jnp.where(other == score_t, tie, 0.0))
    sel_t = jnp.where((rank_t < min(N_SEL, n_slc)) & (jrow < n_slc), 1.0, 0.0)
    sel = jnp.concatenate([sel_t, jnp.zeros((LANES - n_rank, tq), F32)], axis=0).T.astype(BF16)

    m_sc[...] = jnp.full(m_sc.shape, -jnp.inf, F32)
    l_sc[...] = jnp.zeros(l_sc.shape, F32)
    acc_sc[...] = jnp.zeros(acc_sc.shape, F32)
    n_kb = t_len // kb
    brow = lax.broadcasted_iota(I32, (LANES, kb), 0)
    blane = lax.broadcasted_iota(I32, (tq, kb), 1)
    bpos = qt * tq + lax.broadcasted_iota(I32, (tq, kb), 0)
    for cb in range(n_kb):
        @pl.when(cb * kb <= qt * tq)
        def _(cb=cb):
            key = cb * kb + blane
            expand = jnp.where(((cb * kb + lax.broadcasted_iota(I32, (LANES, kb), 1)) >> SLC_SHIFT) == brow,
                               1.0, 0.0).astype(BF16)
            selm = _dot(sel, expand)
            bias = jnp.where((selm > 0.5) & (key <= bpos), 0.0, NEG)
            s = _dot_nt(q_all, ksb_sc[cb * kb:(cb + 1) * kb, :]).reshape(GROUP, tq, kb) + bias[None]
            s_sc[:, cb * kb:(cb + 1) * kb] = s.reshape(rows, kb)
            m_sc[...] = jnp.maximum(m_sc[...], _lane_tile_reduce(s, jnp.maximum).reshape(rows, LANES))

    m_row = jnp.max(m_sc[...], axis=1, keepdims=True)
    for cb in range(n_kb):
        @pl.when(cb * kb <= qt * tq)
        def _(cb=cb):
            p = jnp.exp2(s_sc[:, cb * kb:(cb + 1) * kb] - m_row)
            l_sc[...] += _lane_tile_reduce(p, jnp.add)
            acc_sc[...] += _dot(p.astype(BF16), vsb_sc[cb * kb:(cb + 1) * kb, :])

    o_slc = acc_sc[...] * (1.0 / jnp.sum(l_sc[...], axis=1, keepdims=True))

    start = pl.multiple_of(jnp.maximum(qt * tq + tq - win_keys, 0), PAGE)
    wlane = lax.broadcasted_iota(I32, (tq, win_keys), 1)
    wdiff = (qt * tq + lax.broadcasted_iota(I32, (tq, win_keys), 0)) - (start + wlane)
    wbias = jnp.where((wdiff >= 0) & (wdiff < WINDOW), 0.0, NEG)
    sw = _dot_nt(q_all, kwb_sc[pl.ds(start, win_keys), :]).reshape(GROUP, tq, win_keys) + wbias[None]
    pw = jnp.exp2(sw - jnp.max(sw, axis=2, keepdims=True))
    lw = jnp.sum(pw, axis=2, keepdims=True).reshape(rows, 1)
    o_win = _dot(pw.reshape(rows, win_keys).astype(BF16), vwb_sc[pl.ds(start, win_keys), :]) * (1.0 / lw)

    gates = g_ref[...]
    for g in range(GROUP):
        r = slice(g * tq, (g + 1) * tq)
        o = (gates[:, g:g + 1] * o_cmp[r] + gates[:, GROUP + g:GROUP + g + 1] * o_slc[r]
             + gates[:, 2 * GROUP + g:2 * GROUP + g + 1] * o_win[r])
        o_ref[:, g * HEAD_DIM:(g + 1) * HEAD_DIM] = o.astype(o_ref.dtype)


def _attn_prompt(z, gates, kc, vc):
    bsz, t_len, _ = z.shape
    n_slc = -(-t_len // SLC_LEN)
    assert t_len % PAGE == 0 and n_slc <= LANES and kc.shape[1] <= LANES
    n_cmp_rows = kc.shape[1]
    assert n_cmp_rows == LANES, "compressed keys are laid out on one 128-lane tile"
    kv0 = COL_KV // HEAD_DIM

    def kv_spec(kind):
        return pl.BlockSpec((None, t_len, HEAD_DIM), lambda b, k, t: (b, 0, kv0 + kind * N_KV + k))

    cmp_spec = pl.BlockSpec((None, n_cmp_rows, HEAD_DIM), lambda b, k, t: (b, 0, k))
    q_blk0 = COL_Q // (GROUP * HEAD_DIM)
    kb = min(SLC_KEY_BLOCK, t_len)
    win_keys = min(WINDOW + PAGE, t_len)
    assert t_len % kb == 0
    rows = GROUP * PAGE
    kv_scratch = pltpu.VMEM((t_len, HEAD_DIM), BF16)
    return pl.pallas_call(
        functools.partial(_attn_prompt_kernel, n_slc=n_slc, kb=kb, win_keys=win_keys),
        out_shape=jax.ShapeDtypeStruct((bsz, t_len, ATTN_WIDTH), BF16),
        grid=(bsz, N_KV, t_len // PAGE),
        in_specs=[pl.BlockSpec((None, PAGE, GROUP * HEAD_DIM), lambda b, k, t: (b, t, q_blk0 + k)),
                  cmp_spec, cmp_spec,
                  kv_spec(2), kv_spec(3), kv_spec(4), kv_spec(5),
                  pl.BlockSpec((None, PAGE, LANES), lambda b, k, t: (b, t, k))],
        out_specs=pl.BlockSpec((None, PAGE, GROUP * HEAD_DIM), lambda b, k, t: (b, t, k)),
        scratch_shapes=[kv_scratch, kv_scratch, kv_scratch, kv_scratch,
                        pltpu.VMEM((rows, t_len), F32),
                        pltpu.VMEM((rows, LANES), F32),
                        pltpu.VMEM((rows, LANES), F32),
                        pltpu.VMEM((rows, HEAD_DIM), F32)],
        compiler_params=_cparams(("parallel", "parallel", "arbitrary")),
        name="attn_prompt",
    )(z, kc, vc, z, z, z, z, gates)


def _attn_sample_kernel(*refs, n_pages, n_groups, t_new, n_slc, n_win):
    page_refs = refs[1:1 + n_pages]
    zs_ref, kc_ref, vc_ref, win_ref, g_ref, o_ref, ocmp_sc, sel_sc, m_sc, l_sc, acc_sc = refs[1 + n_pages:]
    pages = [p.reshape(PAGE * 2 * N_KV, HEAD_DIM) for p in page_refs]
    win_rows = win_ref.reshape(n_win * 2 * N_KV, HEAD_DIM)
    grp = pl.program_id(1)
    tp = SUBLANES
    rows = GROUP * tp
    n_cmp_rows = kc_ref.shape[0]
    sel_lanes = sel_sc.shape[2]
    past_len = n_groups * n_pages * PAGE

    def q_heads(k):
        c0 = COL_Q + k * GROUP * HEAD_DIM
        q = zs_ref[:, c0:c0 + GROUP * HEAD_DIM] * SCALE
        return jnp.concatenate([q[:, g * HEAD_DIM:(g + 1) * HEAD_DIM] for g in range(GROUP)],
                               axis=0).astype(BF16)

    @pl.when(grp == 0)
    def _():
        ovb = _overlap_matrix((n_cmp_rows, sel_lanes), n_slc)
        mlane = lax.broadcasted_iota(I32, (rows, n_cmp_rows), 1)
        vbias = jnp.where(mlane >= 1, 0.0, NEG)
        lane = lax.broadcasted_iota(I32, (tp, sel_lanes), 1)
        lane_f = lane.astype(F32)
        tok = lax.broadcasted_iota(I32, (tp, sel_lanes), 0)
        cur = (past_len + tok) >> SLC_SHIFT
        forced = (lane == 0) | (lane == cur) | (lane == cur - 1)
        for k in range(N_KV):
            qk = q_heads(k)
            kc = kc_ref[:, k * HEAD_DIM:(k + 1) * HEAD_DIM]
            vc = vc_ref[:, k * HEAD_DIM:(k + 1) * HEAD_DIM]
            s = _dot_nt(qk, kc) + vbias
            e = jnp.exp(s - jnp.max(s, axis=1, keepdims=True))
            p = e * (1.0 / jnp.sum(e, axis=1, keepdims=True))
            ocmp_sc[k] = _dot(p.astype(BF16), vc)
            psum = p[0:tp]
            for g in range(1, GROUP):
                psum = psum + p[g * tp:(g + 1) * tp]
            imp = _split3_dot(psum, ovb)
            score = jnp.where(lane <= cur, jnp.where(forced, FORCE, imp), -1.0)
            score = jnp.where(lane < n_slc, score, -3e38)
            sel = jnp.zeros((tp, sel_lanes), F32)
            for _ in range(min(N_SEL, n_slc)):
                top = jnp.max(score, axis=1, keepdims=True)
                idx = jnp.min(jnp.where(score == top, lane_f, float(sel_lanes)), axis=1, keepdims=True)
                hit = lane_f == idx
                sel = jnp.where(hit, 1.0, sel)
                score = jnp.where(hit, -3e38, score)
            sel_sc[k] = sel
            _flash_init(m_sc, l_sc, acc_sc, k)

    nk = n_pages * PAGE
    jrow = lax.broadcasted_iota(I32, (sel_lanes, nk), 0)
    klane = lax.broadcasted_iota(I32, (sel_lanes, nk), 1)
    expand = jnp.where(jrow == grp * (nk // SLC_LEN) + (klane >> SLC_SHIFT), 1.0, 0.0).astype(BF16)
    for k in range(N_KV):
        kk = jnp.concatenate([p[pl.ds(k, PAGE, stride=2 * N_KV), :].astype(BF16) for p in pages], axis=0)
        vv = jnp.concatenate([p[pl.ds(N_KV + k, PAGE, stride=2 * N_KV), :].astype(BF16) for p in pages], axis=0)
        selm = _dot(sel_sc[k].astype(BF16), expand)
        bias = jnp.where(selm > 0.5, 0.0, NEG)
        bias = jnp.concatenate([bias] * GROUP, axis=0)
        _flash_update(m_sc, l_sc, acc_sc, k, 0, rows, _dot_nt(q_heads(k), kk) + bias, vv)

    @pl.when(grp == n_groups - 1)
    def _():
        tr = lax.broadcasted_iota(I32, (rows, tp), 0) % tp
        tc = lax.broadcasted_iota(I32, (rows, tp), 1)
        new_bias = jnp.where((tc <= tr) & (tc < t_new), 0.0, NEG)
        wr = lax.broadcasted_iota(I32, (rows, n_win), 0) % tp
        wc = lax.broadcasted_iota(I32, (rows, n_win), 1)
        wdiff = wr + n_win - wc
        win_bias = jnp.where((wdiff >= 0) & (wdiff < WINDOW), 0.0, NEG)
        gates = g_ref[...]
        for k in range(N_KV):
            qk = q_heads(k)

            def new_rows(kind, k=k):
                c0 = COL_KV + kind * KV_WIDTH + k * HEAD_DIM
                return zs_ref[:, c0:c0 + HEAD_DIM].astype(BF16)

            _flash_update(m_sc, l_sc, acc_sc, k, 0, rows, _dot_nt(qk, new_rows(2)) + new_bias, new_rows(3))
            o_slc = acc_sc[k] * (1.0 / l_sc[k])
            kw = win_rows[pl.ds(k, n_win, stride=2 * N_KV), :].astype(BF16)
            vw = win_rows[pl.ds(N_KV + k, n_win, stride=2 * N_KV), :].astype(BF16)
            s1 = _dot_nt(qk, kw) + win_bias
            s2 = _dot_nt(qk, new_rows(4)) + new_bias
            mx = jnp.maximum(jnp.max(s1, axis=1, keepdims=True), jnp.max(s2, axis=1, keepdims=True))
            p1 = jnp.exp(s1 - mx)
            p2 = jnp.exp(s2 - mx)
            den = jnp.sum(p1, axis=1, keepdims=True) + jnp.sum(p2, axis=1, keepdims=True)
            o_win = (_dot(p1.astype(BF16), vw) + _dot(p2.astype(BF16), new_rows(5))) * (1.0 / den)
            o_cmp = ocmp_sc[k]
            for g in range(GROUP):
                r = slice(g * tp, (g + 1) * tp)
                c = k * LANES
                o = (gates[:, c + g:c + g + 1] * o_cmp[r]
                     + gates[:, c + GROUP + g:c + GROUP + g + 1] * o_slc[r]
                     + gates[:, c + 2 * GROUP + g:c + 2 * GROUP + g + 1] * o_win[r])
                h = k * GROUP + g
                o_ref[:, h * HEAD_DIM:(h + 1) * HEAD_DIM] = o.astype(o_ref.dtype)


def _attn_sample(c5, layer, page_table, zs, gates, kc, vc, win5, t_new, n_pages):
    bsz, tot_pages = page_table.shape
    n_groups = tot_pages // n_pages
    past_len = tot_pages * PAGE
    n_slc = -(-(past_len + t_new) // SLC_LEN)
    sel_lanes = -(-n_slc // LANES) * LANES
    n_win = win5.shape[2]
    tp = SUBLANES
    rows = GROUP * tp
    page_specs = [pl.BlockSpec((None, None, PAGE, 2 * N_KV, HEAD_DIM),
                               functools.partial(lambda b, g, pt, i: (layer, pt[b, g * n_pages + i], 0, 1, 0), i=i))
                  for i in range(n_pages)]

    def bmap(b, g, pt):
        return (b, 0, 0)

    grid_spec = pltpu.PrefetchScalarGridSpec(
        num_scalar_prefetch=1,
        grid=(bsz, n_groups),
        in_specs=page_specs + [
            pl.BlockSpec((None, tp, zs.shape[2]), bmap),
            pl.BlockSpec((None, kc.shape[1], KV_WIDTH), bmap),
            pl.BlockSpec((None, vc.shape[1], KV_WIDTH), bmap),
            pl.BlockSpec((None, None, n_win, 2 * N_KV, HEAD_DIM), lambda b, g, pt: (layer, b, 0, 0, 0)),
            pl.BlockSpec((None, tp, N_KV * LANES), bmap)],
        out_specs=pl.BlockSpec((None, tp, ATTN_WIDTH), bmap),
        scratch_shapes=[pltpu.VMEM((N_KV, rows, HEAD_DIM), F32),
                        pltpu.VMEM((N_KV, tp, sel_lanes), F32),
                        pltpu.VMEM((N_KV, rows, LANES), F32),
                        pltpu.VMEM((N_KV, rows, LANES), F32),
                        pltpu.VMEM((N_KV, rows, HEAD_DIM), F32)],
    )
    return pl.pallas_call(
        functools.partial(_attn_sample_kernel, n_pages=n_pages, n_groups=n_groups, t_new=t_new,
                          n_slc=n_slc, n_win=n_win),
        out_shape=jax.ShapeDtypeStruct((bsz, tp, ATTN_WIDTH), BF16),
        grid_spec=grid_spec,
        compiler_params=_cparams(("parallel", "arbitrary")),
        name="attn_sample",
    )(page_table, *([c5] * n_pages), zs, kc, vc, win5, gates)


CONV_HALO = 32
CONV_LANE_CHUNK = 256


def _conv_kernel(*refs, tc, multi_tile):
    if multi_tile:
        u_ref, up_ref, buf_ref, dw_ref, db_ref, lg_ref, lb_ref, o_ref, tail_ref, ext_sc, y_sc = refs
    else:
        u_ref, buf_ref, dw_ref, db_ref, lg_ref, lb_ref, o_ref, tail_ref, ext_sc, y_sc = refs
    t = pl.program_id(1)

    def glu(u):
        return u[:, :CONV_CH] * jax.nn.sigmoid(u[:, CONV_CH:])

    @pl.when(t == 0)
    def _():
        ext_sc[0:CONV_HALO, :] = buf_ref[...]

    if multi_tile:
        @pl.when(t > 0)
        def _():
            ext_sc[0:CONV_HALO, :] = glu(up_ref[...])

    ext_sc[CONV_HALO:CONV_HALO + tc, :] = glu(u_ref[...])
    off = CONV_HALO - (CONV_WIDTH - 1)
    n_ext = CONV_HALO + tc
    for c0 in range(0, CONV_CH, CONV_LANE_CHUNK):
        cs = slice(c0, c0 + CONV_LANE_CHUNK)
        ext = ext_sc[:, cs]
        shifted = [ext] + [pltpu.roll(ext, n_ext - s, 0) for s in range(1, SUBLANES)]
        acc = jnp.broadcast_to(db_ref[:, cs], (tc, CONV_LANE_CHUNK))
        for w in range(CONV_WIDTH):
            s = (off + w) % SUBLANES
            a = off + w - s
            acc = acc + shifted[s][a:a + tc] * dw_ref[w:w + 1, cs]
        y_sc[:, cs] = acc
    y = y_sc[...]
    mu = jnp.mean(y, axis=-1, keepdims=True)
    d = y - mu
    var = jnp.mean(d * d, axis=-1, keepdims=True)
    yn = d * lax.rsqrt(var + LN_EPS) * lg_ref[...] + lb_ref[...]
    o_ref[...] = (yn * jax.nn.sigmoid(yn)).astype(o_ref.dtype)

    @pl.when(t == pl.num_programs(1) - 1)
    def _():
        tail_ref[...] = ext_sc[tc:tc + CONV_HALO, :]


def _conv_group(z, buf, dw_w, dw_b, ln_g, ln_b, tc):
    bsz, t_len, _ = z.shape
    n_t = t_len // tc
    multi_tile = n_t > 1
    assert COL_U == 0 and (not multi_tile or tc % CONV_HALO == 0)
    halo_per_tile = max(tc // CONV_HALO, 1)

    def vec(v):
        return v.reshape(1, CONV_CH)

    cmap = lambda b, t: (0, 0)
    in_specs = [pl.BlockSpec((None, tc, 2 * CONV_CH), lambda b, t: (b, t, 0))]
    args = [z]
    if multi_tile:
        in_specs.append(pl.BlockSpec((None, CONV_HALO, 2 * CONV_CH),
                                     lambda b, t: (b, jnp.maximum(t * halo_per_tile - 1, 0), 0)))
        args.append(z)
    in_specs += [pl.BlockSpec((None, CONV_HALO, CONV_CH), lambda b, t: (b, 0, 0)),
                 pl.BlockSpec((CONV_WIDTH, CONV_CH), cmap),
                 pl.BlockSpec((1, CONV_CH), cmap), pl.BlockSpec((1, CONV_CH), cmap),
                 pl.BlockSpec((1, CONV_CH), cmap)]
    args += [buf, dw_w, vec(dw_b), vec(ln_g), vec(ln_b)]
    return pl.pallas_call(
        functools.partial(_conv_kernel, tc=tc, multi_tile=multi_tile),
        out_shape=(jax.ShapeDtypeStruct((bsz, t_len, CONV_CH), BF16),
                   jax.ShapeDtypeStruct((bsz, CONV_HALO, CONV_CH), F32)),
        grid=(bsz, n_t),
        in_specs=in_specs,
        out_specs=(pl.BlockSpec((None, tc, CONV_CH), lambda b, t: (b, t, 0)),
                   pl.BlockSpec((None, CONV_HALO, CONV_CH), lambda b, t: (b, 0, 0))),
        scratch_shapes=[pltpu.VMEM((CONV_HALO + tc, CONV_CH), F32), pltpu.VMEM((tc, CONV_CH), F32)],
        compiler_params=_cparams(("parallel", "arbitrary")),
        name="conv_group",
    )(*args)


def _rope_tables(pos):
    half = ROT_DIM // 2
    inv = ROPE_THETA ** (-jnp.arange(half, dtype=F32) / half)
    ang = pos.astype(F32)[:, None] * inv[None, :]
    cos, sin = jnp.cos(ang), jnp.sin(ang)
    n = pos.shape[0]
    zeros = jnp.zeros((n, half), F32)
    rest = HEAD_DIM - ROT_DIM
    c = jnp.concatenate([cos, cos, jnp.ones((n, rest), F32)], axis=1)
    s1 = jnp.concatenate([zeros, sin, jnp.zeros((n, rest), F32)], axis=1)
    s2 = jnp.concatenate([-sin, zeros, jnp.zeros((n, rest), F32)], axis=1)
    return c, s1, s2


def _prep_in_weights(w_in):
    depth, d, _ = w_in.shape
    kv_end = ATTN_WIDTH + KV_COLS
    gate_end = kv_end + 3 * N_HEADS
    w_main = jnp.concatenate([w_in[:, :, gate_end:], w_in[:, :, :kv_end]], axis=2).astype(BF16)
    wg = w_in[:, :, kv_end:gate_end].reshape(depth, d, 3, N_KV, GROUP).transpose(0, 1, 3, 2, 4)
    wg = jnp.pad(wg.reshape(depth, d, N_KV, 3 * GROUP), ((0, 0), (0, 0), (0, 0), (0, LANES - 3 * GROUP)))
    return w_main, wg.reshape(depth, d, N_KV * LANES).astype(BF16)


def _prep_cmp_weights(cmp_pe_l, cmp_w1_l, cmp_w2_l, cmp_b2_l):
    pef = jnp.pad(cmp_pe_l.reshape(2, 1, CMP_LEN * HEAD_DIM), ((0, 0), (0, SUBLANES - 1), (0, 0))).astype(BF16)
    w1 = cmp_w1_l.astype(BF16).reshape(2, CMP_LEN // CMP_STRIDE, CMP_STRIDE * HEAD_DIM, CMP_HIDDEN)
    return pef, w1, cmp_w2_l.astype(BF16), cmp_b2_l.reshape(2, 1, HEAD_DIM)


def _largest_tile(n, unit, cap):
    best = None
    for t in range(unit, min(n, cap) + 1, unit):
        if n % t == 0:
            best = t
    assert best is not None, (n, unit, cap)
    return best


def _dense_tail(x, attn, conv, layer, w_out_b, norm_ffn_l, w_gate_b, w_up_b, w_down_b, tm):
    d = x.shape[1]
    d_ff = w_gate_b.shape[2]
    h = _outproj(attn, conv, w_out_b, layer, x, tm, 512)
    hn = _rmsnorm(h, norm_ffn_l, BF16)
    act = _ffn_up(hn, w_gate_b, w_up_b, layer, tm, _largest_tile(d_ff, 2 * LANES, 512))
    return _ffn_down(act, w_down_b, layer, h, min(tm, 512), min(d, 512), d_ff)


def kernel(x_prompt, x_sample, cache_kv, cache_win, state_conv, page_table, norm_mix, w_in, cmp_pe, cmp_w1,
           cmp_w2, cmp_b2, conv_dw_w, conv_dw_b, conv_ln_g, conv_ln_b, w_out, norm_ffn, w_gate, w_up, w_down,
           norm_final):
    depth = w_in.shape[0]
    bsz, seq, d = x_prompt.shape
    dec_b, dec_t, _ = x_sample.shape
    n_pool, page_size = cache_kv.shape[1], cache_kv.shape[2]
    tot_pages = page_table.shape[1]
    past_len = tot_pages * page_size
    assert page_size == PAGE and seq % PAGE == 0 and dec_t <= SUBLANES
    assert conv_dw_w.shape[2] == CONV_CH and d == ATTN_WIDTH + CONV_CH
    pages_per_group = min(16, tot_pages, seq // PAGE)
    assert tot_pages % pages_per_group == 0 and (seq // PAGE) % pages_per_group == 0
    tp = SUBLANES
    ms = dec_b * tp

    tm_p = min(1024, seq)
    tabs_p = _rope_tables(jnp.arange(seq, dtype=I32))
    tabs_s = tuple(jnp.tile(t, (dec_b, 1)) for t in _rope_tables(past_len + jnp.arange(tp, dtype=I32)))

    hp = x_prompt.reshape(bsz * seq, d)
    hs = jnp.pad(x_sample, ((0, 0), (0, tp - dec_t), (0, 0))).reshape(ms, d)
    cache5 = cache_kv.reshape(depth, n_pool, page_size, 4 * N_KV, HEAD_DIM)
    win5 = cache_win.reshape(depth, dec_b, cache_win.shape[2], 2 * N_KV, HEAD_DIM)
    zero_buf = jnp.zeros((bsz, CONV_HALO, CONV_CH), F32)
    tm_dense = min(1024, bsz * seq)

    w_main, wg = _prep_in_weights(w_in)
    w_out_b, w_gate_b, w_up_b, w_down_b = (w.astype(BF16) for w in (w_out, w_gate, w_up, w_down))

    z_layers, convp, kvs, wins, convs = [], [], [], [], []
    for l in range(depth):
        cw = _prep_cmp_weights(cmp_pe[l], cmp_w1[l], cmp_w2[l], cmp_b2[l])

        xn, gates = _norm_gates(hp, norm_mix[l], wg, l)
        z, zc = _inproj(xn, w_main, l, tabs_p, seq // tm_p, tm_p)
        z_layers.append(z)
        z3 = z.reshape(bsz, seq, z.shape[1])
        kc, vc = _compress_prompt(zc.reshape(bsz, seq, 2 * N_KV, HEAD_DIM), cw, pages_per_group)
        attn = _attn_prompt(z3, gates.reshape(bsz, seq, N_KV * LANES), kc, vc)
        conv, tail = _conv_group(z3, zero_buf, conv_dw_w[l], conv_dw_b[l], conv_ln_g[l], conv_ln_b[l], PAGE)
        hp = _dense_tail(hp, attn.reshape(bsz * seq, ATTN_WIDTH), conv.reshape(bsz * seq, CONV_CH), l, w_out_b,
                         norm_ffn[l], w_gate_b, w_up_b, w_down_b, tm_dense)
        convp.append(tail[:, CONV_HALO - (CONV_WIDTH - 1):])

        xn_s, gates_s = _norm_gates(hs, norm_mix[l], wg, l)
        zs, _ = _inproj(xn_s, w_main, l, tabs_s, 1, ms)
        zs3 = zs.reshape(dec_b, tp, zs.shape[1])
        kc_s, vc_s = _compress_sample(cache5, l, page_table, cw, pages_per_group)
        attn_s = _attn_sample(cache5, l, page_table, zs3, gates_s.reshape(dec_b, tp, N_KV * LANES),
                              kc_s, vc_s, win5, dec_t, pages_per_group)
        buf_s = jnp.pad(state_conv[l], ((0, 0), (CONV_HALO - (CONV_WIDTH - 1), 0), (0, 0)))
        conv_s, tail_s = _conv_group(zs3, buf_s, conv_dw_w[l], conv_dw_b[l], conv_ln_g[l], conv_ln_b[l], tp)
        hs = _dense_tail(hs, attn_s.reshape(ms, ATTN_WIDTH), conv_s.reshape(ms, CONV_CH), l, w_out_b, norm_ffn[l],
                         w_gate_b, w_up_b, w_down_b, ms)
        kvs.append(zs3[:, :dec_t, COL_KV:COL_KV + 4 * KV_WIDTH].reshape(dec_b, dec_t, 4, N_KV, HEAD_DIM))
        new_win_s = zs3[:, :dec_t, COL_KV + 4 * KV_WIDTH:COL_KV + KV_COLS].reshape(dec_b, dec_t, 2, N_KV, HEAD_DIM)
        win_all = jnp.concatenate([cache_win[l], new_win_s], axis=1)
        wins.append(win_all[:, win_all.shape[1] - min(WINDOW, win_all.shape[1]):])
        new_glu = tail_s[:, CONV_HALO - tp:CONV_HALO - tp + dec_t]
        ext_s = jnp.concatenate([state_conv[l], new_glu], axis=1)
        convs.append(ext_s[:, ext_s.shape[1] - (CONV_WIDTH - 1):])

    y_prompt = _rmsnorm(hp, norm_final, F32).reshape(bsz, seq, d)
    y_sample = _rmsnorm(hs, norm_final, F32).reshape(dec_b, tp, d)[:, :dec_t]
    keep = min(WINDOW, seq)
    tiles_per_batch = seq // keep
    assert seq % keep == 0
    kv_rows = _rows_out(z_layers, bsz * tiles_per_batch, keep, COL_KV, 4 * N_KV, lambda i: i)
    win_rows = _rows_out(z_layers, bsz, keep, COL_KV + 4 * KV_WIDTH, 2 * N_KV,
                         lambda i: i * tiles_per_batch + tiles_per_batch - 1)
    kv_prompt = kv_rows.reshape(depth, bsz, seq, 4, N_KV, HEAD_DIM)
    win_prompt = win_rows.reshape(depth, bsz, keep, 2, N_KV, HEAD_DIM)
    return (y_prompt, y_sample, kv_prompt, win_prompt, jnp.stack(convp),
            jnp.stack(kvs), jnp.stack(wins), jnp.stack(convs))
```

```python
import functools
import math

import jax
import jax.numpy as jnp
from jax import lax
from jax.experimental import pallas as pl
from jax.experimental.pallas import tpu as pltpu

F32 = jnp.float32
BF16 = jnp.bfloat16
I32 = jnp.int32

N_HEADS = 16
N_KV = 4
HEAD_DIM = 128
GROUP = N_HEADS // N_KV
ATTN_WIDTH = N_HEADS * HEAD_DIM
KV_WIDTH = N_KV * HEAD_DIM
CONV_CH = 2048
CONV_WIDTH = 31
ROT_DIM = HEAD_DIM // 4
ROPE_THETA = 500000.0
CMP_LEN = 32
CMP_STRIDE = 16
CMP_HIDDEN = 2 * HEAD_DIM
SLC_LEN = 64
SLC_SHIFT = 6
N_SEL = 16
WINDOW = 512
SCALE = HEAD_DIM ** -0.5
LOG2E = 1.0 / math.log(2.0)
SLC_KEY_BLOCK = 512
NEG = -1e30
FORCE = 1e9
RMS_EPS = 1e-6
LN_EPS = 1e-5

LANES = 128
SUBLANES = 8
VMEM_LIMIT = 56 * 1024 * 1024

PAGE = 128
CHUNKS_PER_PAGE = PAGE // CMP_STRIDE
COL_U = 0
COL_Q = 2 * CONV_CH
COL_KV = COL_Q + ATTN_WIDTH
KV_COLS = 6 * KV_WIDTH
PROJ_TN = 512


def _cparams(sem):
    return pltpu.CompilerParams(dimension_semantics=sem, vmem_limit_bytes=VMEM_LIMIT)


def _dot(a, b):
    return jnp.dot(a, b, preferred_element_type=F32)


def _dot_nt(a, b):
    return lax.dot_general(a, b, (((1,), (1,)), ((), ())), preferred_element_type=F32)


def _split3_dot(x, m_bf16):
    hi = x.astype(BF16)
    r1 = x - hi.astype(F32)
    mid = r1.astype(BF16)
    lo = (r1 - mid.astype(F32)).astype(BF16)
    return _dot(hi, m_bf16) + _dot(mid, m_bf16) + _dot(lo, m_bf16)


def _rmsnorm_kernel(x_ref, g_ref, o_ref):
    x = x_ref[...]
    ms = jnp.mean(x * x, axis=-1, keepdims=True)
    o_ref[...] = (x * lax.rsqrt(ms + RMS_EPS) * g_ref[...]).astype(o_ref.dtype)


def _rmsnorm(x, g, out_dtype):
    m, d = x.shape
    tm = min(m, 256)
    return pl.pallas_call(
        _rmsnorm_kernel,
        out_shape=jax.ShapeDtypeStruct((m, d), out_dtype),
        grid=(m // tm,),
        in_specs=[pl.BlockSpec((tm, d), lambda i: (i, 0)),
                  pl.BlockSpec((1, d), lambda i: (0, 0))],
        out_specs=pl.BlockSpec((tm, d), lambda i: (i, 0)),
        compiler_params=_cparams(("parallel",)),
        name="rmsnorm",
    )(x, g.reshape(1, d))


def _inproj_kernel(a_ref, w_ref, c_ref, s1_ref, s2_ref, z_ref, zc_ref):
    j = pl.program_id(1)
    n_heads_tile = PROJ_TN // HEAD_DIM
    acc = _dot(a_ref[...], w_ref[...])
    q0 = COL_Q // PROJ_TN
    k0 = COL_KV // PROJ_TN
    is_rope = ((j >= q0) & (j <= k0)) | (j == k0 + 2) | (j == k0 + 4)

    @pl.when(is_rope)
    def _():
        c = c_ref[...]
        s1 = s1_ref[...]
        s2 = s2_ref[...]
        rotated = []
        for h in range(n_heads_tile):
            x = acc[:, h * HEAD_DIM:(h + 1) * HEAD_DIM]
            rotated.append(x * c + pltpu.roll(x, ROT_DIM // 2, 1) * s1
                           + pltpu.roll(x, HEAD_DIM - ROT_DIM // 2, 1) * s2)
            z_ref[:, h * HEAD_DIM:(h + 1) * HEAD_DIM] = rotated[h]

        @pl.when(j == k0)
        def _():
            for h in range(n_heads_tile):
                zc_ref[:, h, :] = rotated[h]

    @pl.when(jnp.logical_not(is_rope))
    def _():
        z_ref[...] = acc

        @pl.when(j == k0 + 1)
        def _():
            for h in range(n_heads_tile):
                zc_ref[:, n_heads_tile + h, :] = acc[:, h * HEAD_DIM:(h + 1) * HEAD_DIM]


def _inproj(xn, w, layer, tabs, tab_tiles, tm):
    m, k = xn.shape
    n = w.shape[2]
    c, s1, s2 = tabs
    tab_spec = pl.BlockSpec((tm, HEAD_DIM), lambda i, j: (i % tab_tiles, 0))
    return pl.pallas_call(
        _inproj_kernel,
        out_shape=(jax.ShapeDtypeStruct((m, n), F32),
                   jax.ShapeDtypeStruct((m, 2 * N_KV, HEAD_DIM), F32)),
        grid=(m // tm, n // PROJ_TN),
        in_specs=[pl.BlockSpec((tm, k), lambda i, j: (i, 0)),
                  pl.BlockSpec((None, k, PROJ_TN), lambda i, j: (layer, 0, j)),
                  tab_spec, tab_spec, tab_spec],
        out_specs=(pl.BlockSpec((tm, PROJ_TN), lambda i, j: (i, j)),
                   pl.BlockSpec((tm, 2 * N_KV, HEAD_DIM), lambda i, j: (i, 0, 0))),
        compiler_params=_cparams(("parallel", "arbitrary")),
        name="inproj",
    )(xn, w, c, s1, s2)


def _norm_gates_kernel(x_ref, gn_ref, wg_ref, xn_ref, g_ref):
    x = x_ref[...]
    ms = jnp.mean(x * x, axis=-1, keepdims=True)
    xn = (x * lax.rsqrt(ms + RMS_EPS) * gn_ref[...]).astype(xn_ref.dtype)
    xn_ref[...] = xn
    g_ref[...] = jax.nn.sigmoid(_dot(xn, wg_ref[...]))


def _norm_gates(x, g_norm, wg, layer):
    m, d = x.shape
    n = wg.shape[2]
    tm = min(m, 256)
    return pl.pallas_call(
        _norm_gates_kernel,
        out_shape=(jax.ShapeDtypeStruct((m, d), BF16), jax.ShapeDtypeStruct((m, n), F32)),
        grid=(m // tm,),
        in_specs=[pl.BlockSpec((tm, d), lambda i: (i, 0)),
                  pl.BlockSpec((1, d), lambda i: (0, 0)),
                  pl.BlockSpec((None, d, n), lambda i: (layer, 0, 0))],
        out_specs=(pl.BlockSpec((tm, d), lambda i: (i, 0)), pl.BlockSpec((tm, n), lambda i: (i, 0))),
        compiler_params=_cparams(("parallel",)),
        name="norm_gates",
    )(x, g_norm.reshape(1, d), wg)


def _rows_out_kernel(*refs, depth):
    z_refs = refs[:depth]
    o_ref = refs[depth]
    layer = pl.program_id(0)
    for li, z_ref in enumerate(z_refs):
        @pl.when(layer == li)
        def _(z_ref=z_ref):
            for r in range(o_ref.shape[1]):
                o_ref[:, r, :] = z_ref[:, r * HEAD_DIM:(r + 1) * HEAD_DIM]


def _rows_out(zs, n_tiles, tm, col0, n_head_rows, row_tile_of):
    depth = len(zs)
    width = n_head_rows * HEAD_DIM
    assert col0 % width == 0
    in_specs = [pl.BlockSpec((tm, width),
                             functools.partial(lambda l, i, li: (jnp.where(l == li, row_tile_of(i), 0), col0 // width),
                                               li=li))
                for li in range(depth)]
    return pl.pallas_call(
        functools.partial(_rows_out_kernel, depth=depth),
        out_shape=jax.ShapeDtypeStruct((depth, n_tiles * tm, n_head_rows, HEAD_DIM), F32),
        grid=(depth, n_tiles),
        in_specs=in_specs,
        out_specs=pl.BlockSpec((None, tm, n_head_rows, HEAD_DIM), lambda l, i: (l, i, 0, 0)),
        compiler_params=_cparams(("arbitrary", "arbitrary")),
        name="rows_out",
    )(*zs)


def _outproj_kernel(a1_ref, a2_ref, w1_ref, w2_ref, r_ref, o_ref):
    o_ref[...] = r_ref[...] + _dot(a1_ref[...], w1_ref[...]) + _dot(a2_ref[...], w2_ref[...])


def _outproj(a1, a2, w, layer, res, tm, tn):
    m, k1 = a1.shape
    k2 = a2.shape[1]
    n = w.shape[2]
    assert k1 == k2 and w.shape[1] == k1 + k2
    return pl.pallas_call(
        _outproj_kernel,
        out_shape=jax.ShapeDtypeStruct((m, n), F32),
        grid=(m // tm, n // tn),
        in_specs=[pl.BlockSpec((tm, k1), lambda i, j: (i, 0)),
                  pl.BlockSpec((tm, k2), lambda i, j: (i, 0)),
                  pl.BlockSpec((None, k1, tn), lambda i, j: (layer, 0, j)),
                  pl.BlockSpec((None, k2, tn), lambda i, j: (layer, 1, j)),
                  pl.BlockSpec((tm, tn), lambda i, j: (i, j))],
        out_specs=pl.BlockSpec((tm, tn), lambda i, j: (i, j)),
        compiler_params=_cparams(("parallel", "arbitrary")),
        name="outproj",
    )(a1, a2, w, w, res)


def _ffn_up_kernel(a_ref, wg_ref, wu_ref, o_ref):
    a = a_ref[...]
    gate = _dot(a, wg_ref[...])
    up = _dot(a, wu_ref[...])
    o_ref[...] = (gate * jax.nn.sigmoid(gate) * up).astype(o_ref.dtype)


def _ffn_up(a, wg, wu, layer, tm, tn):
    m, k = a.shape
    n = wg.shape[2]
    return pl.pallas_call(
        _ffn_up_kernel,
        out_shape=jax.ShapeDtypeStruct((m, n), BF16),
        grid=(m // tm, n // tn),
        in_specs=[pl.BlockSpec((tm, k), lambda i, j: (i, 0)),
                  pl.BlockSpec((None, k, tn), lambda i, j: (layer, 0, j)),
                  pl.BlockSpec((None, k, tn), lambda i, j: (layer, 0, j))],
        out_specs=pl.BlockSpec((tm, tn), lambda i, j: (i, j)),
        compiler_params=_cparams(("parallel", "arbitrary")),
        name="ffn_up",
    )(a, wg, wu)


def _ffn_down_kernel(a_ref, w_ref, r_ref, o_ref, acc_ref):
    kk = pl.program_id(2)

    @pl.when(kk == 0)
    def _():
        acc_ref[...] = r_ref[...]

    acc_ref[...] += _dot(a_ref[...], w_ref[...])

    @pl.when(kk == pl.num_programs(2) - 1)
    def _():
        o_ref[...] = acc_ref[...]


def _ffn_down(a, w, layer, res, tm, tn, tk):
    m, k = a.shape
    n = w.shape[2]
    return pl.pallas_call(
        _ffn_down_kernel,
        out_shape=jax.ShapeDtypeStruct((m, n), F32),
        grid=(m // tm, n // tn, k // tk),
        in_specs=[pl.BlockSpec((tm, tk), lambda i, j, kk: (i, kk)),
                  pl.BlockSpec((None, tk, tn), lambda i, j, kk: (layer, kk, j)),
                  pl.BlockSpec((tm, tn), lambda i, j, kk: (i, j))],
        out_specs=pl.BlockSpec((tm, tn), lambda i, j, kk: (i, j)),
        scratch_shapes=[pltpu.VMEM((tm, tn), F32)],
        compiler_params=_cparams(("parallel", "parallel", "arbitrary")),
        name="ffn_down",
    )(a, w, res)


def _gelu_tanh(x):
    return 0.5 * x * (1.0 + jnp.tanh(math.sqrt(2.0 / math.pi) * (x + 0.044715 * (x * x * x))))


def _compress_kernel(*refs, n_pages, n_prefetch):
    page_refs = refs[n_prefetch:n_prefetch + n_pages]
    pef_ref, w1_ref, w2_ref, b2_ref, kc_ref, vc_ref, carry_ref = refs[n_prefetch + n_pages:]
    grp = pl.program_id(1)
    n_chunks = n_pages * CHUNKS_PER_PAGE
    half_k = CMP_STRIDE * HEAD_DIM

    @pl.when(grp == 0)
    def _():
        carry_ref[...] = jnp.zeros_like(carry_ref)

    rows = N_KV * n_chunks
    pairs = CHUNKS_PER_PAGE // 2
    lo = lax.broadcasted_iota(I32, (pairs, CMP_STRIDE, 2 * N_KV, HEAD_DIM), 2) < N_KV
    xk, xv = [], []
    for p in page_refs:
        x = p[...].reshape(pairs, 2, CMP_STRIDE, 2 * N_KV, HEAD_DIM)
        even, odd = x[:, 0], x[:, 1]
        kp = jnp.where(lo, even, pltpu.roll(odd, N_KV, 2))
        vp = jnp.where(lo, pltpu.roll(even, N_KV, 2), odd)
        for t, dst in ((kp, xk), (vp, xv)):
            dst.append(jnp.concatenate([t[:, j] for j in range(CMP_STRIDE)], axis=-1)
                       .reshape(pairs * 2 * N_KV, CMP_STRIDE * HEAD_DIM))
    row = lax.broadcasted_iota(I32, (rows, CMP_HIDDEN), 0)
    for kind, parts, out_ref in ((0, xk, kc_ref), (1, xv, vc_ref)):
        x_all = jnp.concatenate(parts, axis=0).astype(BF16)
        a0 = _dot(x_all, w1_ref[kind, 0])
        a1 = _dot(x_all, w1_ref[kind, 1])
        pe = pef_ref[kind]
        pe_term = (_dot(pe[:, :half_k], w1_ref[kind, 0]) + _dot(pe[:, half_k:], w1_ref[kind, 1]))[0:1, :]
        shifted = pltpu.roll(jnp.where(row >= rows - N_KV, carry_ref[kind], a0), N_KV, 0)
        pre = shifted + a1 + pe_term
        out_ref[...] = (_dot(_gelu_tanh(pre).astype(BF16), w2_ref[kind]) + b2_ref[kind]).astype(out_ref.dtype)
        carry_ref[kind] = a0


def _compress_prompt(zc, cw, n_pages):
    bsz, t_len = zc.shape[:2]
    specs = [pl.BlockSpec((None, PAGE, 2 * N_KV, HEAD_DIM),
                          functools.partial(lambda b, g, i: (b, g * n_pages + i, 0, 0), i=i))
             for i in range(n_pages)]
    return _compress(zc, specs, bsz, (t_len // PAGE) // n_pages, n_pages, cw)


def _compress_sample(c5, layer, page_table, cw, n_pages):
    bsz, tot_pages = page_table.shape
    specs = [pl.BlockSpec((None, None, PAGE, 2 * N_KV, HEAD_DIM),
                          functools.partial(lambda b, g, pt, i: (layer, pt[b, g * n_pages + i], 0, 0, 0), i=i))
             for i in range(n_pages)]
    return _compress(c5, specs, bsz, tot_pages // n_pages, n_pages, cw, prefetch=(page_table,))


def _compress(src, page_specs, n_batch, n_groups, n_pages, cw, prefetch=()):
    pef, w1, w2, b2 = cw
    n_chunks = n_pages * CHUNKS_PER_PAGE

    def const_spec(x):
        return pl.BlockSpec(x.shape, lambda b, g, *pt: (0,) * x.ndim)

    out_spec = pl.BlockSpec((None, N_KV * n_chunks, HEAD_DIM), lambda b, g, *pt: (b, g, 0))
    out_shape = jax.ShapeDtypeStruct((n_batch, n_groups * N_KV * n_chunks, HEAD_DIM), BF16)
    grid_spec = pltpu.PrefetchScalarGridSpec(
        num_scalar_prefetch=len(prefetch),
        grid=(n_batch, n_groups),
        in_specs=list(page_specs) + [const_spec(pef), const_spec(w1), const_spec(w2), const_spec(b2)],
        out_specs=(out_spec, out_spec),
        scratch_shapes=[pltpu.VMEM((2, N_KV * n_chunks, CMP_HIDDEN), F32)],
    )
    kc, vc = pl.pallas_call(
        functools.partial(_compress_kernel, n_pages=n_pages, n_prefetch=len(prefetch)),
        out_shape=(out_shape, out_shape),
        grid_spec=grid_spec,
        compiler_params=_cparams(("parallel", "arbitrary")),
        name="compress",
    )(*prefetch, *([src] * n_pages), pef, w1, w2, b2)
    return (kc.reshape(n_batch, n_groups * n_chunks, KV_WIDTH), vc.reshape(n_batch, n_groups * n_chunks, KV_WIDTH))


def _flash_init(m_sc, l_sc, acc_sc, br):
    m_sc[br] = jnp.full(m_sc.shape[1:], -jnp.inf, F32)
    l_sc[br] = jnp.zeros(l_sc.shape[1:], F32)
    acc_sc[br] = jnp.zeros(acc_sc.shape[1:], F32)


def _flash_update(m_sc, l_sc, acc_sc, br, r0, nr, s, v):
    m_old = m_sc[br, r0:r0 + nr, :]
    m_new = jnp.maximum(m_old, jnp.max(s, axis=1, keepdims=True))
    alpha = jnp.exp(m_old - m_new)
    p = jnp.exp(s - (m_new if s.shape[1] == LANES else m_new[:, 0:1]))
    l_sc[br, r0:r0 + nr, :] = alpha * l_sc[br, r0:r0 + nr, :] + jnp.sum(p, axis=1, keepdims=True)
    acc_sc[br, r0:r0 + nr, :] = alpha * acc_sc[br, r0:r0 + nr, :] + _dot(p.astype(BF16), v)
    m_sc[br, r0:r0 + nr, :] = m_new


def _overlap_matrix(shape, n_slc):
    m = lax.broadcasted_iota(I32, shape, 0)
    j = lax.broadcasted_iota(I32, shape, 1)
    start = (m - 1) * CMP_STRIDE
    ov = (m >= 1) & (start < j * SLC_LEN + SLC_LEN) & (start + CMP_LEN > j * SLC_LEN) & (j < n_slc)
    return jnp.where(ov, 1.0, 0.0).astype(BF16)


def _lane_tile_reduce(x, op):
    out = x[..., 0:LANES]
    for i in range(1, x.shape[-1] // LANES):
        out = op(out, x[..., i * LANES:(i + 1) * LANES])
    return out


def _attn_prompt_kernel(q_ref, kc_ref, vc_ref, ks_ref, vs_ref, kw_ref, vw_ref, g_ref, o_ref,
                        ksb_sc, vsb_sc, kwb_sc, vwb_sc, s_sc, m_sc, l_sc, acc_sc, *, n_slc, kb, win_keys):
    qt = pl.program_id(2)
    tq = PAGE
    rows = GROUP * tq
    t_len = ks_ref.shape[0]

    @pl.when(qt == 0)
    def _():
        ksb_sc[...] = ks_ref[...].astype(BF16)
        vsb_sc[...] = vs_ref[...].astype(BF16)
        kwb_sc[...] = kw_ref[...].astype(BF16)
        vwb_sc[...] = vw_ref[...].astype(BF16)

    q = q_ref[...] * (SCALE * LOG2E)
    q_all = jnp.concatenate([q[:, g * HEAD_DIM:(g + 1) * HEAD_DIM] for g in range(GROUP)],
                            axis=0).astype(BF16)
    row = lax.broadcasted_iota(I32, (tq, LANES), 0)
    lane = lax.broadcasted_iota(I32, (tq, LANES), 1)
    pos = qt * tq + row

    valid = (lane >= 1) & (lane * CMP_STRIDE + (CMP_LEN - CMP_STRIDE - 1) <= pos)
    vbias = jnp.where(valid, 0.0, NEG)
    vf = jnp.where(valid, 1.0, 0.0)
    s3 = _dot_nt(q_all, kc_ref[...]).reshape(GROUP, tq, LANES) + vbias[None]
    e = jnp.exp2(s3 - jnp.max(s3, axis=2, keepdims=True)) * vf[None]
    denom = jnp.sum(e, axis=2, keepdims=True)
    p3 = e * (1.0 / jnp.maximum(denom, 1e-30))
    o_cmp = _dot(p3.reshape(rows, LANES).astype(BF16), vc_ref[...])
    psum = p3[0] + p3[1] + p3[2] + p3[3]

    imp = _split3_dot(psum, _overlap_matrix((LANES, LANES), n_slc))
    cur = pos >> SLC_SHIFT
    forced = (lane == 0) | (lane == cur) | (lane == cur - 1)
    score = jnp.where(lane <= cur, jnp.where(forced, FORCE, imp), -1.0)
    score = jnp.where(lane < n_slc, score, -2.0)
    n_rank = -(-n_slc // SUBLANES) * SUBLANES
    score_t = score.T[0:n_rank]
    jrow = lax.broadcasted_iota(I32, (n_rank, tq), 0)
    rank_t = jnp.zeros((n_rank, tq), F32)
    for jp in range(n_slc):
        other = score_t[jp:jp + 1, :]
        tie = jnp.where(jrow > jp, 1.0, 0.0)
        rank_t = rank_t + jnp.where(other > score_t, 1.0, jnp.where(other == score_t, tie, 0.0))
    sel_t = jnp.where((rank_t < min(N_SEL, n_slc)) & (jrow < n_slc), 1.0, 0.0)
    sel = jnp.concatenate([sel_t, jnp.zeros((LANES - n_rank, tq), F32)], axis=0).T.astype(BF16)

    m_sc[...] = jnp.full(m_sc.shape, -jnp.inf, F32)
    l_sc[...] = jnp.zeros(l_sc.shape, F32)
    acc_sc[...] = jnp.zeros(acc_sc.shape, F32)
    n_kb = t_len // kb
    brow = lax.broadcasted_iota(I32, (LANES, kb), 0)
    blane = lax.broadcasted_iota(I32, (tq, kb), 1)
    bpos = qt * tq + lax.broadcasted_iota(I32, (tq, kb), 0)
    for cb in range(n_kb):
        @pl.when(cb * kb <= qt * tq)
        def _(cb=cb):
            key = cb * kb + blane
            expand = jnp.where(((cb * kb + lax.broadcasted_iota(I32, (LANES, kb), 1)) >> SLC_SHIFT) == brow,
                               1.0, 0.0).astype(BF16)
            selm = _dot(sel, expand)
            bias = jnp.where((selm > 0.5) & (key <= bpos), 0.0, NEG)
            s = _dot_nt(q_all, ksb_sc[cb * kb:(cb + 1) * kb, :]).reshape(GROUP, tq, kb) + bias[None]
            s_sc[:, cb * kb:(cb + 1) * kb] = s.reshape(rows, kb)
            m_sc[...] = jnp.maximum(m_sc[...], _lane_tile_reduce(s, jnp.maximum).reshape(rows, LANES))

    m_row = jnp.max(m_sc[...], axis=1, keepdims=True)
    for cb in range(n_kb):
        @pl.when(cb * kb <= qt * tq)
        def _(cb=cb):
            p = jnp.exp2(s_sc[:, cb * kb:(cb + 1) * kb] - m_row)
            l_sc[...] += _lane_tile_reduce(p, jnp.add)
            acc_sc[...] += _dot(p.astype(BF16), vsb_sc[cb * kb:(cb + 1) * kb, :])

    o_slc = acc_sc[...] * (1.0 / jnp.sum(l_sc[...], axis=1, keepdims=True))

    start = pl.multiple_of(jnp.maximum(qt * tq + tq - win_keys, 0), PAGE)
    wlane = lax.broadcasted_iota(I32, (tq, win_keys), 1)
    wdiff = (qt * tq + lax.broadcasted_iota(I32, (tq, win_keys), 0)) - (start + wlane)
    wbias = jnp.where((wdiff >= 0) & (wdiff < WINDOW), 0.0, NEG)
    sw = _dot_nt(q_all, kwb_sc[pl.ds(start, win_keys), :]).reshape(GROUP, tq, win_keys) + wbias[None]
    pw = jnp.exp2(sw - jnp.max(sw, axis=2, keepdims=True))
    lw = jnp.sum(pw, axis=2, keepdims=True).reshape(rows, 1)
    o_win = _dot(pw.reshape(rows, win_keys).astype(BF16), vwb_sc[pl.ds(start, win_keys), :]) * (1.0 / lw)

    gates = g_ref[...]
    for g in range(GROUP):
        r = slice(g * tq, (g + 1) * tq)
        o = (gates[:, g:g + 1] * o_cmp[r] + gates[:, GROUP + g:GROUP + g + 1] * o_slc[r]
             + gates[:, 2 * GROUP + g:2 * GROUP + g + 1] * o_win[r])
        o_ref[:, g * HEAD_DIM:(g + 1) * HEAD_DIM] = o.astype(o_ref.dtype)


def _attn_prompt(z, gates, kc, vc):
    bsz, t_len, _ = z.shape
    n_slc = -(-t_len // SLC_LEN)
    assert t_len % PAGE == 0 and n_slc <= LANES and kc.shape[1] <= LANES
    n_cmp_rows = kc.shape[1]
    assert n_cmp_rows == LANES, "compressed keys are laid out on one 128-lane tile"
    kv0 = COL_KV // HEAD_DIM

    def kv_spec(kind):
        return pl.BlockSpec((None, t_len, HEAD_DIM), lambda b, k, t: (b, 0, kv0 + kind * N_KV + k))

    cmp_spec = pl.BlockSpec((None, n_cmp_rows, HEAD_DIM), lambda b, k, t: (b, 0, k))
    q_blk0 = COL_Q // (GROUP * HEAD_DIM)
    kb = min(SLC_KEY_BLOCK, t_len)
    win_keys = min(WINDOW + PAGE, t_len)
    assert t_len % kb == 0
    rows = GROUP * PAGE
    kv_scratch = pltpu.VMEM((t_len, HEAD_DIM), BF16)
    return pl.pallas_call(
        functools.partial(_attn_prompt_kernel, n_slc=n_slc, kb=kb, win_keys=win_keys),
        out_shape=jax.ShapeDtypeStruct((bsz, t_len, ATTN_WIDTH), BF16),
        grid=(bsz, N_KV, t_len // PAGE),
        in_specs=[pl.BlockSpec((None, PAGE, GROUP * HEAD_DIM), lambda b, k, t: (b, t, q_blk0 + k)),
                  cmp_spec, cmp_spec,
                  kv_spec(2), kv_spec(3), kv_spec(4), kv_spec(5),
                  pl.BlockSpec((None, PAGE, LANES), lambda b, k, t: (b, t, k))],
        out_specs=pl.BlockSpec((None, PAGE, GROUP * HEAD_DIM), lambda b, k, t: (b, t, k)),
        scratch_shapes=[kv_scratch, kv_scratch, kv_scratch, kv_scratch,
                        pltpu.VMEM((rows, t_len), F32),
                        pltpu.VMEM((rows, LANES), F32),
                        pltpu.VMEM((rows, LANES), F32),
                        pltpu.VMEM((rows, HEAD_DIM), F32)],
        compiler_params=_cparams(("parallel", "parallel", "arbitrary")),
        name="attn_prompt",
    )(z, kc, vc, z, z, z, z, gates)


def _attn_sample_kernel(*refs, n_pages, n_groups, t_new, n_slc, n_win):
    page_refs = refs[1:1 + n_pages]
    zs_ref, kc_ref, vc_ref, win_ref, g_ref, o_ref, ocmp_sc, sel_sc, m_sc, l_sc, acc_sc = refs[1 + n_pages:]
    pages = [p.reshape(PAGE * 2 * N_KV, HEAD_DIM) for p in page_refs]
    win_rows = win_ref.reshape(n_win * 2 * N_KV, HEAD_DIM)
    grp = pl.program_id(1)
    tp = SUBLANES
    rows = GROUP * tp
    n_cmp_rows = kc_ref.shape[0]
    sel_lanes = sel_sc.shape[2]
    past_len = n_groups * n_pages * PAGE

    def q_heads(k):
        c0 = COL_Q + k * GROUP * HEAD_DIM
        q = zs_ref[:, c0:c0 + GROUP * HEAD_DIM] * SCALE
        return jnp.concatenate([q[:, g * HEAD_DIM:(g + 1) * HEAD_DIM] for g in range(GROUP)],
                               axis=0).astype(BF16)

    @pl.when(grp == 0)
    def _():
        ovb = _overlap_matrix((n_cmp_rows, sel_lanes), n_slc)
        mlane = lax.broadcasted_iota(I32, (rows, n_cmp_rows), 1)
        vbias = jnp.where(mlane >= 1, 0.0, NEG)
        lane = lax.broadcasted_iota(I32, (tp, sel_lanes), 1)
        lane_f = lane.astype(F32)
        tok = lax.broadcasted_iota(I32, (tp, sel_lanes), 0)
        cur = (past_len + tok) >> SLC_SHIFT
        forced = (lane == 0) | (lane == cur) | (lane == cur - 1)
        for k in range(N_KV):
            qk = q_heads(k)
            kc = kc_ref[:, k * HEAD_DIM:(k + 1) * HEAD_DIM]
            vc = vc_ref[:, k * HEAD_DIM:(k + 1) * HEAD_DIM]
            s = _dot_nt(qk, kc) + vbias
            e = jnp.exp(s - jnp.max(s, axis=1, keepdims=True))
            p = e * (1.0 / jnp.sum(e, axis=1, keepdims=True))
            ocmp_sc[k] = _dot(p.astype(BF16), vc)
            psum = p[0:tp]
            for g in range(1, GROUP):
                psum = psum + p[g * tp:(g + 1) * tp]
            imp = _split3_dot(psum, ovb)
            score = jnp.where(lane <= cur, jnp.where(forced, FORCE, imp), -1.0)
            score = jnp.where(lane < n_slc, score, -3e38)
            sel = jnp.zeros((tp, sel_lanes), F32)
            for _ in range(min(N_SEL, n_slc)):
                top = jnp.max(score, axis=1, keepdims=True)
                idx = jnp.min(jnp.where(score == top, lane_f, float(sel_lanes)), axis=1, keepdims=True)
                hit = lane_f == idx
                sel = jnp.where(hit, 1.0, sel)
                score = jnp.where(hit, -3e38, score)
            sel_sc[k] = sel
            _flash_init(m_sc, l_sc, acc_sc, k)

    nk = n_pages * PAGE
    jrow = lax.broadcasted_iota(I32, (sel_lanes, nk), 0)
    klane = lax.broadcasted_iota(I32, (sel_lanes, nk), 1)
    expand = jnp.where(jrow == grp * (nk // SLC_LEN) + (klane >> SLC_SHIFT), 1.0, 0.0).astype(BF16)
    for k in range(N_KV):
        kk = jnp.concatenate([p[pl.ds(k, PAGE, stride=2 * N_KV), :].astype(BF16) for p in pages], axis=0)
        vv = jnp.concatenate([p[pl.ds(N_KV + k, PAGE, stride=2 * N_KV), :].astype(BF16) for p in pages], axis=0)
        selm = _dot(sel_sc[k].astype(BF16), expand)
        bias = jnp.where(selm > 0.5, 0.0, NEG)
        bias = jnp.concatenate([bias] * GROUP, axis=0)
        _flash_update(m_sc, l_sc, acc_sc, k, 0, rows, _dot_nt(q_heads(k), kk) + bias, vv)

    @pl.when(grp == n_groups - 1)
    def _():
        tr = lax.broadcasted_iota(I32, (rows, tp), 0) % tp
        tc = lax.broadcasted_iota(I32, (rows, tp), 1)
        new_bias = jnp.where((tc <= tr) & (tc < t_new), 0.0, NEG)
        wr = lax.broadcasted_iota(I32, (rows, n_win), 0) % tp
        wc = lax.broadcasted_iota(I32, (rows, n_win), 1)
        wdiff = wr + n_win - wc
        win_bias = jnp.where((wdiff >= 0) & (wdiff < WINDOW), 0.0, NEG)
        gates = g_ref[...]
        for k in range(N_KV):
            qk = q_heads(k)

            def new_rows(kind, k=k):
                c0 = COL_KV + kind * KV_WIDTH + k * HEAD_DIM
                return zs_ref[:, c0:c0 + HEAD_DIM].astype(BF16)

            _flash_update(m_sc, l_sc, acc_sc, k, 0, rows, _dot_nt(qk, new_rows(2)) + new_bias, new_rows(3))
            o_slc = acc_sc[k] * (1.0 / l_sc[k])
            kw = win_rows[pl.ds(k, n_win, stride=2 * N_KV), :].astype(BF16)
            vw = win_rows[pl.ds(N_KV + k, n_win, stride=2 * N_KV), :].astype(BF16)
            s1 = _dot_nt(qk, kw) + win_bias
            s2 = _dot_nt(qk, new_rows(4)) + new_bias
            mx = jnp.maximum(jnp.max(s1, axis=1, keepdims=True), jnp.max(s2, axis=1, keepdims=True))
            p1 = jnp.exp(s1 - mx)
            p2 = jnp.exp(s2 - mx)
            den = jnp.sum(p1, axis=1, keepdims=True) + jnp.sum(p2, axis=1, keepdims=True)
            o_win = (_dot(p1.astype(BF16), vw) + _dot(p2.astype(BF16), new_rows(5))) * (1.0 / den)
            o_cmp = ocmp_sc[k]
            for g in range(GROUP):
                r = slice(g * tp, (g + 1) * tp)
                c = k * LANES
                o = (gates[:, c + g:c + g + 1] * o_cmp[r]
                     + gates[:, c + GROUP + g:c + GROUP + g + 1] * o_slc[r]
                     + gates[:, c + 2 * GROUP + g:c + 2 * GROUP + g + 1] * o_win[r])
                h = k * GROUP + g
                o_ref[:, h * HEAD_DIM:(h + 1) * HEAD_DIM] = o.astype(o_ref.dtype)


def _attn_sample(c5, layer, page_table, zs, gates, kc, vc, win5, t_new, n_pages):
    bsz, tot_pages = page_table.shape
    n_groups = tot_pages // n_pages
    past_len = tot_pages * PAGE
    n_slc = -(-(past_len + t_new) // SLC_LEN)
    sel_lanes = -(-n_slc // LANES) * LANES
    n_win = win5.shape[2]
    tp = SUBLANES
    rows = GROUP * tp
    page_specs = [pl.BlockSpec((None, None, PAGE, 2 * N_KV, HEAD_DIM),
                               functools.partial(lambda b, g, pt, i: (layer, pt[b, g * n_pages + i], 0, 1, 0), i=i))
                  for i in range(n_pages)]

    def bmap(b, g, pt):
        return (b, 0, 0)

    grid_spec = pltpu.PrefetchScalarGridSpec(
        num_scalar_prefetch=1,
        grid=(bsz, n_groups),
        in_specs=page_specs + [
            pl.BlockSpec((None, tp, zs.shape[2]), bmap),
            pl.BlockSpec((None, kc.shape[1], KV_WIDTH), bmap),
            pl.BlockSpec((None, vc.shape[1], KV_WIDTH), bmap),
            pl.BlockSpec((None, None, n_win, 2 * N_KV, HEAD_DIM), lambda b, g, pt: (layer, b, 0, 0, 0)),
            pl.BlockSpec((None, tp, N_KV * LANES), bmap)],
        out_specs=pl.BlockSpec((None, tp, ATTN_WIDTH), bmap),
        scratch_shapes=[pltpu.VMEM((N_KV, rows, HEAD_DIM), F32),
                        pltpu.VMEM((N_KV, tp, sel_lanes), F32),
                        pltpu.VMEM((N_KV, rows, LANES), F32),
                        pltpu.VMEM((N_KV, rows, LANES), F32),
                        pltpu.VMEM((N_KV, rows, HEAD_DIM), F32)],
    )
    return pl.pallas_call(
        functools.partial(_attn_sample_kernel, n_pages=n_pages, n_groups=n_groups, t_new=t_new,
                          n_slc=n_slc, n_win=n_win),
        out_shape=jax.ShapeDtypeStruct((bsz, tp, ATTN_WIDTH), BF16),
        grid_spec=grid_spec,
        compiler_params=_cparams(("parallel", "arbitrary")),
        name="attn_sample",
    )(page_table, *([c5] * n_pages), zs, kc, vc, win5, gates)


CONV_HALO = 32
CONV_LANE_CHUNK = 256


def _conv_kernel(*refs, tc, multi_tile):
    if multi_tile:
        u_ref, up_ref, buf_ref, dw_ref, db_ref, lg_ref, lb_ref, o_ref, tail_ref, ext_sc, y_sc = refs
    else:
        u_ref, buf_ref, dw_ref, db_ref, lg_ref, lb_ref, o_ref, tail_ref, ext_sc, y_sc = refs
    t = pl.program_id(1)

    def glu(u):
        return u[:, :CONV_CH] * jax.nn.sigmoid(u[:, CONV_CH:])

    @pl.when(t == 0)
    def _():
        ext_sc[0:CONV_HALO, :] = buf_ref[...]

    if multi_tile:
        @pl.when(t > 0)
        def _():
            ext_sc[0:CONV_HALO, :] = glu(up_ref[...])

    ext_sc[CONV_HALO:CONV_HALO + tc, :] = glu(u_ref[...])
    off = CONV_HALO - (CONV_WIDTH - 1)
    n_ext = CONV_HALO + tc
    for c0 in range(0, CONV_CH, CONV_LANE_CHUNK):
        cs = slice(c0, c0 + CONV_LANE_CHUNK)
        ext = ext_sc[:, cs]
        shifted = [ext] + [pltpu.roll(ext, n_ext - s, 0) for s in range(1, SUBLANES)]
        acc = jnp.broadcast_to(db_ref[:, cs], (tc, CONV_LANE_CHUNK))
        for w in range(CONV_WIDTH):
            s = (off + w) % SUBLANES
            a = off + w - s
            acc = acc + shifted[s][a:a + tc] * dw_ref[w:w + 1, cs]
        y_sc[:, cs] = acc
    y = y_sc[...]
    mu = jnp.mean(y, axis=-1, keepdims=True)
    d = y - mu
    var = jnp.mean(d * d, axis=-1, keepdims=True)
    yn = d * lax.rsqrt(var + LN_EPS) * lg_ref[...] + lb_ref[...]
    o_ref[...] = (yn * jax.nn.sigmoid(yn)).astype(o_ref.dtype)

    @pl.when(t == pl.num_programs(1) - 1)
    def _():
        tail_ref[...] = ext_sc[tc:tc + CONV_HALO, :]


def _conv_group(z, buf, dw_w, dw_b, ln_g, ln_b, tc):
    bsz, t_len, _ = z.shape
    n_t = t_len // tc
    multi_tile = n_t > 1
    assert COL_U == 0 and (not multi_tile or tc % CONV_HALO == 0)
    halo_per_tile = max(tc // CONV_HALO, 1)

    def vec(v):
        return v.reshape(1, CONV_CH)

    cmap = lambda b, t: (0, 0)
    in_specs = [pl.BlockSpec((None, tc, 2 * CONV_CH), lambda b, t: (b, t, 0))]
    args = [z]
    if multi_tile:
        in_specs.append(pl.BlockSpec((None, CONV_HALO, 2 * CONV_CH),
                                     lambda b, t: (b, jnp.maximum(t * halo_per_tile - 1, 0), 0)))
        args.append(z)
    in_specs += [pl.BlockSpec((None, CONV_HALO, CONV_CH), lambda b, t: (b, 0, 0)),
                 pl.BlockSpec((CONV_WIDTH, CONV_CH), cmap),
                 pl.BlockSpec((1, CONV_CH), cmap), pl.BlockSpec((1, CONV_CH), cmap),
                 pl.BlockSpec((1, CONV_CH), cmap)]
    args += [buf, dw_w, vec(dw_b), vec(ln_g), vec(ln_b)]
    return pl.pallas_call(
        functools.partial(_conv_kernel, tc=tc, multi_tile=multi_tile),
        out_shape=(jax.ShapeDtypeStruct((bsz, t_len, CONV_CH), BF16),
                   jax.ShapeDtypeStruct((bsz, CONV_HALO, CONV_CH), F32)),
        grid=(bsz, n_t),
        in_specs=in_specs,
        out_specs=(pl.BlockSpec((None, tc, CONV_CH), lambda b, t: (b, t, 0)),
                   pl.BlockSpec((None, CONV_HALO, CONV_CH), lambda b, t: (b, 0, 0))),
        scratch_shapes=[pltpu.VMEM((CONV_HALO + tc, CONV_CH), F32), pltpu.VMEM((tc, CONV_CH), F32)],
        compiler_params=_cparams(("parallel", "arbitrary")),
        name="conv_group",
    )(*args)


def _rope_tables(pos):
    half = ROT_DIM // 2
    inv = ROPE_THETA ** (-jnp.arange(half, dtype=F32) / half)
    ang = pos.astype(F32)[:, None] * inv[None, :]
    cos, sin = jnp.cos(ang), jnp.sin(ang)
    n = pos.shape[0]
    zeros = jnp.zeros((n, half), F32)
    rest = HEAD_DIM - ROT_DIM
    c = jnp.concatenate([cos, cos, jnp.ones((n, rest), F32)], axis=1)
    s1 = jnp.concatenate([zeros, sin, jnp.zeros((n, rest), F32)], axis=1)
    s2 = jnp.concatenate([-sin, zeros, jnp.zeros((n, rest), F32)], axis=1)
    return c, s1, s2


def _prep_in_weights(w_in):
    depth, d, _ = w_in.shape
    kv_end = ATTN_WIDTH + KV_COLS
    gate_end = kv_end + 3 * N_HEADS
    w_main = jnp.concatenate([w_in[:, :, gate_end:], w_in[:, :, :kv_end]], axis=2).astype(BF16)
    wg = w_in[:, :, kv_end:gate_end].reshape(depth, d, 3, N_KV, GROUP).transpose(0, 1, 3, 2, 4)
    wg = jnp.pad(wg.reshape(depth, d, N_KV, 3 * GROUP), ((0, 0), (0, 0), (0, 0), (0, LANES - 3 * GROUP)))
    return w_main, wg.reshape(depth, d, N_KV * LANES).astype(BF16)


def _prep_cmp_weights(cmp_pe_l, cmp_w1_l, cmp_w2_l, cmp_b2_l):
    pef = jnp.pad(cmp_pe_l.reshape(2, 1, CMP_LEN * HEAD_DIM), ((0, 0), (0, SUBLANES - 1), (0, 0))).astype(BF16)
    w1 = cmp_w1_l.astype(BF16).reshape(2, CMP_LEN // CMP_STRIDE, CMP_STRIDE * HEAD_DIM, CMP_HIDDEN)
    return pef, w1, cmp_w2_l.astype(BF16), cmp_b2_l.reshape(2, 1, HEAD_DIM)


def _largest_tile(n, unit, cap):
    best = None
    for t in range(unit, min(n, cap) + 1, unit):
        if n % t == 0:
            best = t
    assert best is not None, (n, unit, cap)
    return best


def _dense_tail(x, attn, conv, layer, w_out_b, norm_ffn_l, w_gate_b, w_up_b, w_down_b, tm):
    d = x.shape[1]
    d_ff = w_gate_b.shape[2]
    h = _outproj(attn, conv, w_out_b, layer, x, tm, 512)
    hn = _rmsnorm(h, norm_ffn_l, BF16)
    act = _ffn_up(hn, w_gate_b, w_up_b, layer, _largest_tile(x.shape[0], tm, 2 * tm),
                  _largest_tile(d_ff, 2 * LANES, 512))
    return _ffn_down(act, w_down_b, layer, h, min(tm, 512), min(d, 512), d_ff)


def kernel(x_prompt, x_sample, cache_kv, cache_win, state_conv, page_table, norm_mix, w_in, cmp_pe, cmp_w1,
           cmp_w2, cmp_b2, conv_dw_w, conv_dw_b, conv_ln_g, conv_ln_b, w_out, norm_ffn, w_gate, w_up, w_down,
           norm_final):
    depth = w_in.shape[0]
    bsz, seq, d = x_prompt.shape
    dec_b, dec_t, _ = x_sample.shape
    n_pool, page_size = cache_kv.shape[1], cache_kv.shape[2]
    tot_pages = page_table.shape[1]
    past_len = tot_pages * page_size
    assert page_size == PAGE and seq % PAGE == 0 and dec_t <= SUBLANES
    assert conv_dw_w.shape[2] == CONV_CH and d == ATTN_WIDTH + CONV_CH
    pages_per_group = min(16, tot_pages, seq // PAGE)
    assert tot_pages % pages_per_group == 0 and (seq // PAGE) % pages_per_group == 0
    tp = SUBLANES
    ms = dec_b * tp

    tm_p = min(1024, seq)
    tabs_p = _rope_tables(jnp.arange(seq, dtype=I32))
    tabs_s = tuple(jnp.tile(t, (dec_b, 1)) for t in _rope_tables(past_len + jnp.arange(tp, dtype=I32)))

    hp = x_prompt.reshape(bsz * seq, d)
    hs = jnp.pad(x_sample, ((0, 0), (0, tp - dec_t), (0, 0))).reshape(ms, d)
    cache5 = cache_kv.reshape(depth, n_pool, page_size, 4 * N_KV, HEAD_DIM)
    win5 = cache_win.reshape(depth, dec_b, cache_win.shape[2], 2 * N_KV, HEAD_DIM)
    zero_buf = jnp.zeros((bsz, CONV_HALO, CONV_CH), F32)
    tm_dense = min(1024, bsz * seq)

    w_main, wg = _prep_in_weights(w_in)
    w_out_b, w_gate_b, w_up_b, w_down_b = (w.astype(BF16) for w in (w_out, w_gate, w_up, w_down))

    z_layers, convp, kvs, wins, convs = [], [], [], [], []
    for l in range(depth):
        cw = _prep_cmp_weights(cmp_pe[l], cmp_w1[l], cmp_w2[l], cmp_b2[l])

        xn, gates = _norm_gates(hp, norm_mix[l], wg, l)
        z, zc = _inproj(xn, w_main, l, tabs_p, seq // tm_p, tm_p)
        z_layers.append(z)
        z3 = z.reshape(bsz, seq, z.shape[1])
        kc, vc = _compress_prompt(zc.reshape(bsz, seq, 2 * N_KV, HEAD_DIM), cw, pages_per_group)
        attn = _attn_prompt(z3, gates.reshape(bsz, seq, N_KV * LANES), kc, vc)
        conv, tail = _conv_group(z3, zero_buf, conv_dw_w[l], conv_dw_b[l], conv_ln_g[l], conv_ln_b[l], PAGE)
        hp = _dense_tail(hp, attn.reshape(bsz * seq, ATTN_WIDTH), conv.reshape(bsz * seq, CONV_CH), l, w_out_b,
                         norm_ffn[l], w_gate_b, w_up_b, w_down_b, tm_dense)
        convp.append(tail[:, CONV_HALO - (CONV_WIDTH - 1):])

        xn_s, gates_s = _norm_gates(hs, norm_mix[l], wg, l)
        zs, _ = _inproj(xn_s, w_main, l, tabs_s, 1, ms)
        zs3 = zs.reshape(dec_b, tp, zs.shape[1])
        kc_s, vc_s = _compress_sample(cache5, l, page_table, cw, pages_per_group)
        attn_s = _attn_sample(cache5, l, page_table, zs3, gates_s.reshape(dec_b, tp, N_KV * LANES),
                              kc_s, vc_s, win5, dec_t, pages_per_group)
        buf_s = jnp.pad(state_conv[l], ((0, 0), (CONV_HALO - (CONV_WIDTH - 1), 0), (0, 0)))
        conv_s, tail_s = _conv_group(zs3, buf_s, conv_dw_w[l], conv_dw_b[l], conv_ln_g[l], conv_ln_b[l], tp)
        hs = _dense_tail(hs, attn_s.reshape(ms, ATTN_WIDTH), conv_s.reshape(ms, CONV_CH), l, w_out_b, norm_ffn[l],
                         w_gate_b, w_up_b, w_down_b, ms)
        kvs.append(zs3[:, :dec_t, COL_KV:COL_KV + 4 * KV_WIDTH].reshape(dec_b, dec_t, 4, N_KV, HEAD_DIM))
        new_win_s = zs3[:, :dec_t, COL_KV + 4 * KV_WIDTH:COL_KV + KV_COLS].reshape(dec_b, dec_t, 2, N_KV, HEAD_DIM)
        win_all = jnp.concatenate([cache_win[l], new_win_s], axis=1)
        wins.append(win_all[:, win_all.shape[1] - min(WINDOW, win_all.shape[1]):])
        new_glu = tail_s[:, CONV_HALO - tp:CONV_HALO - tp + dec_t]
        ext_s = jnp.concatenate([state_conv[l], new_glu], axis=1)
        convs.append(ext_s[:, ext_s.shape[1] - (CONV_WIDTH - 1):])

    y_prompt = _rmsnorm(hp, norm_final, F32).reshape(bsz, seq, d)
    y_sample = _rmsnorm(hs, norm_final, F32).reshape(dec_b, tp, d)[:, :dec_t]
    keep = min(WINDOW, seq)
    tiles_per_batch = seq // keep
    assert seq % keep == 0
    kv_rows = _rows_out(z_layers, bsz * tiles_per_batch, keep, COL_KV, 4 * N_KV, lambda i: i)
    win_rows = _rows_out(z_layers, bsz, keep, COL_KV + 4 * KV_WIDTH, 2 * N_KV,
                         lambda i: i * tiles_per_batch + tiles_per_batch - 1)
    kv_prompt = kv_rows.reshape(depth, bsz, seq, 4, N_KV, HEAD_DIM)
    win_prompt = win_rows.reshape(depth, bsz, keep, 2, N_KV, HEAD_DIM)
    return (y_prompt, y_sample, kv_prompt, win_prompt, jnp.stack(convp),
            jnp.stack(kvs), jnp.stack(wins), jnp.stack(convs))
```
